```python
import jax
import jax.numpy as jnp
from jax import lax
import numpy as np

D_MODEL = 4096
BATCH = 1
SEQ = 8192
DEPTH = 2

RET_HEADS = 8
RET_QK_DIM = 256
RET_V_DIM = 256
RET_CHUNK = 128
MOBA_HEADS = 16
MOBA_HEAD_DIM = 128
MOBA_BLOCK = 256
MOBA_TOPK = 3
MOBA_QCHUNK = 64
CONV_WIDTH = 3
FFN_HIDDEN = ((8 * D_MODEL + 767) // 768) * 256
EPS = 1e-6
RET_QK_W = RET_HEADS * RET_QK_DIM
RET_V_W = RET_HEADS * RET_V_DIM
MOBA_W = MOBA_HEADS * MOBA_HEAD_DIM
MIX_IN_SPLITS = (RET_QK_W, RET_QK_W, RET_V_W, RET_V_W, MOBA_W, MOBA_W, MOBA_W)
MIX_IN_W = sum(MIX_IN_SPLITS)
MIX_OUT_W = RET_V_W + MOBA_W

kernel_name = 'hybrid_retention_moba_shortconv_block'


def rms_norm(x, g):
    xf = x.astype(jnp.float32)
    y = xf * lax.rsqrt(jnp.mean(xf * xf, axis=-1, keepdims=True) + EPS)
    return (y * g.astype(jnp.float32)).astype(x.dtype)


def modulate(x, g, c_act, w_mod, b_mod):
    mod = c_act @ w_mod + b_mod
    shift, scale, gate = jnp.split(mod[:, None, :], 3, axis=-1)
    h = rms_norm(x, g) * (1.0 + scale) + shift
    return h, gate


def head_norm(o):
    of = o.astype(jnp.float32)
    mu = jnp.mean(of, axis=-1, keepdims=True)
    var = jnp.mean(jnp.square(of - mu), axis=-1, keepdims=True)
    return ((of - mu) * lax.rsqrt(var + EPS)).astype(o.dtype)


def retention_decay_log(n_heads):
    return jnp.log1p(-jnp.exp2(-5.0 - jnp.arange(n_heads, dtype=jnp.float32)))


def chunkwise_retention(q, k, v):
    b, h, L, dk = q.shape
    dv = v.shape[-1]
    C = RET_CHUNK
    nc = L // C
    dt = q.dtype
    log_g = retention_decay_log(h)
    pos = jnp.arange(C, dtype=jnp.float32)
    rel = pos[:, None] - pos[None, :]
    intra = jnp.where(rel >= 0, jnp.exp(log_g[:, None, None] * jnp.maximum(rel, 0.0)), 0.0)
    q_decay = jnp.exp(log_g[:, None] * (pos + 1.0))
    k_decay = jnp.exp(log_g[:, None] * (C - 1.0 - pos))
    chunk_decay = jnp.exp(log_g * C)
    k = k * (dk ** -0.5)
    qc = q.reshape(b, h, nc, C, dk)
    kc = k.reshape(b, h, nc, C, dk)
    vc = v.reshape(b, h, nc, C, dv)
    scores = jnp.einsum('bhcnd,bhcmd->bhcnm', qc, kc) * intra[None, :, None].astype(dt)
    o_inner = jnp.einsum('bhcnm,bhcmv->bhcnv', scores, vc)

    def step(state, inp):
        q_i, k_i, v_i = inp
        o_cross = jnp.einsum('bhnd,bhdv->bhnv', q_i.astype(jnp.float32), state) * q_decay[None, :, :, None]
        kv = jnp.einsum('bhmd,bhmv->bhdv', k_i.astype(jnp.float32) * k_decay[None, :, :, None],
                        v_i.astype(jnp.float32))
        new_state = state * chunk_decay[None, :, None, None] + kv
        return new_state, o_cross.astype(dt)

    state0 = jnp.zeros((b, h, dk, dv), jnp.float32)
    xs = (jnp.moveaxis(qc, 2, 0), jnp.moveaxis(kc, 2, 0), jnp.moveaxis(vc, 2, 0))
    _, o_cross = lax.scan(step, state0, xs)
    o = o_inner + jnp.moveaxis(o_cross, 0, 2)
    return o.reshape(b, h, L, dv)


def alibi_slopes(n_heads):
    return jnp.exp2(-8.0 * jnp.arange(1, n_heads + 1, dtype=jnp.float32) / n_heads)


def moba_attention(q, k, v):
    b, h, L, d = q.shape
    dt = q.dtype
    BS = MOBA_BLOCK
    nb = -(-L // BS)
    Lp = nb * BS
    padw = ((0, 0), (0, 0), (0, Lp - L), (0, 0))
    q = jnp.pad(q, padw)
    k_blocks = jnp.pad(k, padw).reshape(b, h, nb, BS, d)
    v_blocks = jnp.pad(v, padw).reshape(b, h, nb, BS, d)
    k_mean = jnp.mean(k_blocks.astype(jnp.float32), axis=3).astype(dt)
    slopes = alibi_slopes(h)
    k_sel = min(MOBA_TOPK, nb)
    scale = d ** -0.5
    b_idx = jnp.arange(b)[:, None, None, None]
    h_idx = jnp.arange(h)[None, :, None, None]
    blk_pos = jnp.arange(BS)
    Qc = MOBA_QCHUNK
    n_chunks = Lp // Qc

    def chunk(ci):
        start = ci * Qc
        blk = start // BS
        q_c = lax.dynamic_slice_in_dim(q, start, Qc, axis=2)
        t = start + jnp.arange(Qc)
        gate = jnp.einsum('bhqd,bhnd->bhqn', q_c, k_mean).astype(jnp.float32)
        gate = jnp.where(jnp.arange(nb) < blk, gate, -jnp.inf)
        _, idx = lax.top_k(gate, k_sel)
        valid = jnp.arange(k_sel) < blk
        k_g = k_blocks[b_idx, h_idx, idx]
        v_g = v_blocks[b_idx, h_idx, idx]
        s_pos = idx[..., None] * BS + blk_pos
        dist = (t[:, None, None] - s_pos).astype(jnp.float32)
        s_sel = jnp.einsum('bhqd,bhqnkd->bhqnk', q_c, k_g).astype(jnp.float32) * scale \
            - slopes[None, :, None, None, None] * dist
        s_sel = jnp.where(valid[:, None], s_sel, -jnp.inf).reshape(b, h, Qc, k_sel * BS)
        k_own = lax.dynamic_index_in_dim(k_blocks, blk, axis=2, keepdims=False)
        v_own = lax.dynamic_index_in_dim(v_blocks, blk, axis=2, keepdims=False)
        dist_own = (t[:, None] - (blk * BS + blk_pos)[None, :]).astype(jnp.float32)
        s_own = jnp.einsum('bhqd,bhkd->bhqk', q_c, k_own).astype(jnp.float32) * scale \
            - slopes[None, :, None, None] * dist_own
        s_own = jnp.where(dist_own >= 0, s_own, -jnp.inf)
        p = jax.nn.softmax(jnp.concatenate([s_sel, s_own], axis=-1), axis=-1).astype(dt)
        p_sel = p[..., :k_sel * BS].reshape(b, h, Qc, k_sel, BS)
        p_own = p[..., k_sel * BS:]
        return jnp.einsum('bhqnk,bhqnkd->bhqd', p_sel, v_g) + jnp.einsum('bhqk,bhkd->bhqd', p_own, v_own)

    out = lax.map(chunk, jnp.arange(n_chunks))
    out = jnp.moveaxis(out, 0, 2).reshape(b, h, Lp, d)
    return out[:, :, :L]


def retention_moba_mixer(h, w_in, w_out):
    b, L, _ = h.shape
    offs = np.cumsum(MIX_IN_SPLITS)[:-1].tolist()
    rq, rk, rv, rg, mq, mk, mv = jnp.split(h @ w_in, offs, axis=-1)

    def heads(t, n, d):
        return t.reshape(b, L, n, d).transpose(0, 2, 1, 3)

    ret = chunkwise_retention(heads(rq, RET_HEADS, RET_QK_DIM), heads(rk, RET_HEADS, RET_QK_DIM),
                              heads(rv, RET_HEADS, RET_V_DIM))
    ret = head_norm(ret).transpose(0, 2, 1, 3).reshape(b, L, RET_V_W) * jax.nn.silu(rg)
    att = moba_attention(heads(mq, MOBA_HEADS, MOBA_HEAD_DIM), heads(mk, MOBA_HEADS, MOBA_HEAD_DIM),
                         heads(mv, MOBA_HEADS, MOBA_HEAD_DIM))
    att = att.transpose(0, 2, 1, 3).reshape(b, L, MOBA_W)
    return jnp.concatenate([ret, att], axis=-1) @ w_out


def short_conv_mixer(h, w_in, conv_w, w_out):
    b_gate, c_gate, u = jnp.split(h @ w_in, 3, axis=-1)
    z = c_gate * u
    y = lax.conv_general_dilated(z, conv_w, window_strides=(1,), padding=[(CONV_WIDTH - 1, 0)],
                                 dimension_numbers=('NWC', 'WIO', 'NWC'),
                                 feature_group_count=z.shape[-1])
    return (b_gate * y) @ w_out


def swiglu(h, w1, w3, w2):
    return (jax.nn.silu(h @ w1) * (h @ w3)) @ w2


def _normal(key, shape, scale):
    return jax.random.normal(key, shape, jnp.float32) * scale


def setup_inputs(seed: int = 0) -> dict:
    key = jax.random.key(seed)
    ks = iter(jax.random.split(key, 32))
    D, F = D_MODEL, FFN_HIDDEN

    def gain():
        return 1.0 + _normal(next(ks), (D,), 0.05)

    def mod_w():
        return _normal(next(ks), (D, 3 * D), 0.5 * D ** -0.5)

    def mod_b():
        return _normal(next(ks), (3 * D,), 0.01)

    p = {}
    p['x'] = _normal(next(ks), (BATCH, SEQ, D), 1.0)
    p['c'] = _normal(next(ks), (BATCH, D), 1.0)
    p['l0_mix_norm_g'] = gain()
    p['l0_mix_mod_w'] = mod_w()
    p['l0_mix_mod_b'] = mod_b()
    p['l0_mix_w_in'] = _normal(next(ks), (D, MIX_IN_W), D ** -0.5)
    p['l0_mix_w_out'] = _normal(next(ks), (MIX_OUT_W, D), MIX_OUT_W ** -0.5)
    p['l0_ffn_norm_g'] = gain()
    p['l0_ffn_mod_w'] = mod_w()
    p['l0_ffn_mod_b'] = mod_b()
    p['l0_ffn_w1'] = _normal(next(ks), (D, F), D ** -0.5)
    p['l0_ffn_w3'] = _normal(next(ks), (D, F), D ** -0.5)
    p['l0_ffn_w2'] = _normal(next(ks), (F, D), F ** -0.5)
    p['l1_mix_norm_g'] = gain()
    p['l1_mix_mod_w'] = mod_w()
    p['l1_mix_mod_b'] = mod_b()
    p['l1_mix_w_in'] = _normal(next(ks), (D, 3 * D), D ** -0.5)
    p['l1_mix_conv_w'] = _normal(next(ks), (CONV_WIDTH, 1, D), CONV_WIDTH ** -0.5)
    p['l1_mix_w_out'] = _normal(next(ks), (D, D), D ** -0.5)
    p['l1_ffn_norm_g'] = gain()
    p['l1_ffn_mod_w'] = mod_w()
    p['l1_ffn_mod_b'] = mod_b()
    p['l1_ffn_w1'] = _normal(next(ks), (D, F), D ** -0.5)
    p['l1_ffn_w3'] = _normal(next(ks), (D, F), D ** -0.5)
    p['l1_ffn_w2'] = _normal(next(ks), (F, D), F ** -0.5)
    p['final_norm_g'] = gain()
    return p


def reference(x, c, l0_mix_norm_g, l0_mix_mod_w, l0_mix_mod_b, l0_mix_w_in, l0_mix_w_out,
              l0_ffn_norm_g, l0_ffn_mod_w, l0_ffn_mod_b, l0_ffn_w1, l0_ffn_w3, l0_ffn_w2,
              l1_mix_norm_g, l1_mix_mod_w, l1_mix_mod_b, l1_mix_w_in, l1_mix_conv_w, l1_mix_w_out,
              l1_ffn_norm_g, l1_ffn_mod_w, l1_ffn_mod_b, l1_ffn_w1, l1_ffn_w3, l1_ffn_w2,
              final_norm_g):
    c_act = jax.nn.silu(c)
    mix_norm_g = (l0_mix_norm_g, l1_mix_norm_g)
    mix_mod_w = (l0_mix_mod_w, l1_mix_mod_w)
    mix_mod_b = (l0_mix_mod_b, l1_mix_mod_b)
    ffn_norm_g = (l0_ffn_norm_g, l1_ffn_norm_g)
    ffn_mod_w = (l0_ffn_mod_w, l1_ffn_mod_w)
    ffn_mod_b = (l0_ffn_mod_b, l1_ffn_mod_b)
    ffn_w1 = (l0_ffn_w1, l1_ffn_w1)
    ffn_w3 = (l0_ffn_w3, l1_ffn_w3)
    ffn_w2 = (l0_ffn_w2, l1_ffn_w2)
    for layer in range(DEPTH):
        h, gate = modulate(x, mix_norm_g[layer], c_act, mix_mod_w[layer], mix_mod_b[layer])
        if layer % 2 == 0:
            m = retention_moba_mixer(h, l0_mix_w_in, l0_mix_w_out)
        else:
            m = short_conv_mixer(h, l1_mix_w_in, l1_mix_conv_w, l1_mix_w_out)
        x = x + gate * m
        h, gate = modulate(x, ffn_norm_g[layer], c_act, ffn_mod_w[layer], ffn_mod_b[layer])
        x = x + gate * swiglu(h, ffn_w1[layer], ffn_w3[layer], ffn_w2[layer])
    return rms_norm(x, final_norm_g)
```

```python
import functools

import jax
import jax.numpy as jnp
from jax import lax
from jax.experimental import pallas as pl
from jax.experimental.pallas import tpu as pltpu

D_MODEL = 4096
SEQ = 8192
RET_HEADS = 8
RET_DIM = 256
RET_CHUNK = 128
MOBA_HEADS = 16
MOBA_DIM = 128
MOBA_BLOCK = 256
MOBA_TOPK = 3
FFN_HIDDEN = 11008
EPS = 1e-6
RET_W = RET_HEADS * RET_DIM
MOBA_W = MOBA_HEADS * MOBA_DIM
N_BLOCKS = SEQ // MOBA_BLOCK

VMEM_LIMIT_BYTES = 56 * 1024 * 1024
NEG_BIG = -1e30

BF16 = jnp.bfloat16
F32 = jnp.float32


def _params(*sem):
    return pltpu.CompilerParams(dimension_semantics=sem, vmem_limit_bytes=VMEM_LIMIT_BYTES)


def _silu(v):
    return v * jax.nn.sigmoid(v)


MOD_TN = 512
MOD_KC = 128


def _mod_kernel(c_ref, w_ref, b_ref, o_ref, ca_ref):
    @pl.when(pl.program_id(0) == 0)
    def _():
        ca_ref[...] = _silu(c_ref[...])

    def body(k, acc):
        r = pl.multiple_of(k * MOD_KC, MOD_KC)
        p = w_ref[pl.ds(r, MOD_KC), :] * ca_ref[pl.ds(r, MOD_KC), :]
        return acc + jnp.sum(p.reshape(MOD_KC // 8, 8, MOD_TN), axis=0)

    acc = lax.fori_loop(0, D_MODEL // MOD_KC, body, jnp.zeros((8, MOD_TN), F32))
    o_ref[...] = jnp.sum(acc, axis=0, keepdims=True) + b_ref[...]


def _mod_vector(c_col, w, b):
    n = w.shape[1]
    return pl.pallas_call(
        _mod_kernel,
        grid=(n // MOD_TN,),
        in_specs=[
            pl.BlockSpec((D_MODEL, 1), lambda j: (0, 0)),
            pl.BlockSpec((D_MODEL, MOD_TN), lambda j: (0, j)),
            pl.BlockSpec((1, MOD_TN), lambda j: (0, j)),
        ],
        out_specs=pl.BlockSpec((1, MOD_TN), lambda j: (0, j)),
        out_shape=jax.ShapeDtypeStruct((1, n), F32),
        scratch_shapes=[pltpu.VMEM((D_MODEL, 1), F32)],
        compiler_params=_params("arbitrary"),
        name="adaln_mod",
    )(c_col, w, b.reshape(1, n))


NORM_TM = 256


def _normmod_kernel(x_ref, g_ref, sc_ref, sh_ref, o_ref):
    x = x_ref[...]
    ms = jnp.mean(x * x, axis=-1, keepdims=True)
    y = x * lax.rsqrt(ms + EPS) * g_ref[...]
    o_ref[...] = (y * (1.0 + sc_ref[...]) + sh_ref[...]).astype(o_ref.dtype)


def _norm_kernel(x_ref, g_ref, o_ref):
    x = x_ref[...]
    ms = jnp.mean(x * x, axis=-1, keepdims=True)
    o_ref[...] = (x * lax.rsqrt(ms + EPS) * g_ref[...]).astype(o_ref.dtype)


def _norm_modulate(x, g, scale, shift):
    m, d = x.shape
    row = pl.BlockSpec((NORM_TM, d), lambda i: (i, 0))
    vec = pl.BlockSpec((1, d), lambda i: (0, 0))
    return pl.pallas_call(
        _normmod_kernel,
        grid=(m // NORM_TM,),
        in_specs=[row, vec, vec, vec],
        out_specs=row,
        out_shape=jax.ShapeDtypeStruct((m, d), BF16),
        compiler_params=_params("arbitrary"),
        name="norm_modulate",
    )(x, g.reshape(1, d), scale, shift)


def _final_norm(x, g):
    m, d = x.shape
    row = pl.BlockSpec((NORM_TM, d), lambda i: (i, 0))
    vec = pl.BlockSpec((1, d), lambda i: (0, 0))
    return pl.pallas_call(
        _norm_kernel,
        grid=(m // NORM_TM,),
        in_specs=[row, vec],
        out_specs=row,
        out_shape=jax.ShapeDtypeStruct((m, d), F32),
        compiler_params=_params("arbitrary"),
        name="final_norm",
    )(x, g.reshape(1, d))


def _dot(a, b):
    return jnp.dot(a, b, preferred_element_type=F32)


def _proj_kernel(a_ref, w_ref, o_ref):
    o_ref[...] = _dot(a_ref[...], w_ref[...]).astype(o_ref.dtype)


def _project(a, w, col0, ncols, out_dtype, tm=1024, tn=512, name="proj"):
    m, k = a.shape
    cb = col0 // tn
    return pl.pallas_call(
        _proj_kernel,
        grid=(m // tm, ncols // tn),
        in_specs=[
            pl.BlockSpec((tm, k), lambda i, j: (i, 0)),
            pl.BlockSpec((k, tn), lambda i, j: (0, cb + j)),
        ],
        out_specs=pl.BlockSpec((tm, tn), lambda i, j: (i, j)),
        out_shape=jax.ShapeDtypeStruct((m, ncols), out_dtype),
        compiler_params=_params("arbitrary", "arbitrary"),
        name=name,
    )(a, w)


def _swiglu_kernel(a_ref, w1_ref, w3_ref, o_ref):
    a = a_ref[...]
    h1 = _dot(a, w1_ref[...])
    h3 = _dot(a, w3_ref[...])
    o_ref[...] = (_silu(h1) * h3).astype(o_ref.dtype)


def _swiglu_up(a, w1, w3, tm=1024, tn=256):
    m, k = a.shape
    n = w1.shape[1]
    wspec = pl.BlockSpec((k, tn), lambda i, j: (0, j))
    return pl.pallas_call(
        _swiglu_kernel,
        grid=(m // tm, n // tn),
        in_specs=[pl.BlockSpec((tm, k), lambda i, j: (i, 0)), wspec, wspec],
        out_specs=pl.BlockSpec((tm, tn), lambda i, j: (i, j)),
        out_shape=jax.ShapeDtypeStruct((m, n), BF16),
        compiler_params=_params("arbitrary", "arbitrary"),
        name="swiglu_up",
    )(a, w1, w3)


def _resid_kernel(a_ref, w_ref, x_ref, g_ref, o_ref, acc_ref, *, nk):
    part = _dot(a_ref[...], w_ref[...])
    if nk == 1:
        o_ref[...] = x_ref[...] + g_ref[...] * part
        return
    k = pl.program_id(2)

    @pl.when(k == 0)
    def _():
        acc_ref[...] = part

    @pl.when(jnp.logical_and(k > 0, k < nk - 1))
    def _():
        acc_ref[...] += part

    @pl.when(k == nk - 1)
    def _():
        o_ref[...] = x_ref[...] + g_ref[...] * (acc_ref[...] + part)


def _project_residual(a, w, x, gate, tm=1024, tn=512, tk=None, name="proj_resid"):
    m, k = a.shape
    n = w.shape[1]
    tk = k if tk is None else tk
    nk = k // tk
    return pl.pallas_call(
        functools.partial(_resid_kernel, nk=nk),
        grid=(m // tm, n // tn, nk),
        in_specs=[
            pl.BlockSpec((tm, tk), lambda i, j, kk: (i, kk)),
            pl.BlockSpec((tk, tn), lambda i, j, kk: (kk, j)),
            pl.BlockSpec((tm, tn), lambda i, j, kk: (i, j)),
            pl.BlockSpec((1, tn), lambda i, j, kk: (0, j)),
        ],
        out_specs=pl.BlockSpec((tm, tn), lambda i, j, kk: (i, j)),
        out_shape=jax.ShapeDtypeStruct((m, n), F32),
        scratch_shapes=[pltpu.VMEM((tm, tn), F32)],
        compiler_params=_params("arbitrary", "arbitrary", "arbitrary"),
        name=name,
    )(a, w, x, gate)


CONV_HALO = 16


def _conv_proj_kernel(a_ref, ap_ref, wb_ref, wc_ref, wu_ref, cw_ref, o_ref):
    i = pl.program_id(0)
    a = a_ref[...]
    b = _dot(a, wb_ref[...])
    z = _dot(a, wc_ref[...]) * _dot(a, wu_ref[...])
    ap = ap_ref[...]
    zp = _dot(ap, wc_ref[...]) * _dot(ap, wu_ref[...])
    zp = jnp.where(i > 0, zp, 0.0)
    p1 = zp[CONV_HALO - 1:CONV_HALO, :]
    p2 = zp[CONV_HALO - 2:CONV_HALO - 1, :]
    row = lax.broadcasted_iota(jnp.int32, z.shape, 0)
    z1 = jnp.where(row == 0, p1, pltpu.roll(z, 1, 0))
    z2 = jnp.where(row == 0, p2, jnp.where(row == 1, p1, pltpu.roll(z, 2, 0)))
    cw = cw_ref[...]
    y = cw[2:3, :] * z + cw[1:2, :] * z1 + cw[0:1, :] * z2
    o_ref[...] = (b * y).astype(o_ref.dtype)


def _conv_project(a, w, conv_w, tm=1024, tn=256):
    m, k = a.shape
    n = D_MODEL
    nb = n // tn
    hb = tm // CONV_HALO
    return pl.pallas_call(
        _conv_proj_kernel,
        grid=(m // tm, nb),
        in_specs=[
            pl.BlockSpec((tm, k), lambda i, j: (i, 0)),
            pl.BlockSpec((CONV_HALO, k), lambda i, j: (jnp.maximum(i * hb - 1, 0), 0)),
            pl.BlockSpec((k, tn), lambda i, j: (0, j)),
            pl.BlockSpec((k, tn), lambda i, j: (0, nb + j)),
            pl.BlockSpec((k, tn), lambda i, j: (0, 2 * nb + j)),
            pl.BlockSpec((3, tn), lambda i, j: (0, j)),
        ],
        out_specs=pl.BlockSpec((tm, tn), lambda i, j: (i, j)),
        out_shape=jax.ShapeDtypeStruct((m, n), BF16),
        compiler_params=_params("arbitrary", "arbitrary"),
        name="conv_proj",
    )(a, a, w, w, w, conv_w)


RET_T = 512


def _ret_kernel(lg_ref, q_ref, k_ref, v_ref, g_ref, o_ref, state_ref):
    h = pl.program_id(0)

    @pl.when(pl.program_id(1) == 0)
    def _():
        state_ref[...] = jnp.zeros_like(state_ref)

    c = RET_CHUNK
    lg = lg_ref[h]
    inv_scale = RET_DIM ** -0.5
    ri = lax.broadcasted_iota(jnp.int32, (c, c), 0)
    ci = lax.broadcasted_iota(jnp.int32, (c, c), 1)
    rel = (ri - ci).astype(F32)
    intra = jnp.where(rel >= 0, jnp.exp(lg * jnp.maximum(rel, 0.0)), 0.0) * inv_scale
    pos = lax.broadcasted_iota(jnp.int32, (c, 1), 0).astype(F32)
    q_decay = jnp.exp(lg * (pos + 1.0))
    k_decay = jnp.exp(lg * (c - 1.0 - pos)) * inv_scale
    chunk_decay = jnp.exp(jnp.full((1, 1), lg * c, F32))

    for t in range(RET_T // c):
        sl = slice(t * c, (t + 1) * c)
        q = q_ref[sl, :]
        k = k_ref[sl, :]
        v = v_ref[sl, :]
        state = state_ref[...]
        scores = lax.dot_general(q, k, (((1,), (1,)), ((), ())), preferred_element_type=F32) * intra
        o = _dot(scores.astype(BF16), v)
        o = o + _dot(q, state.astype(BF16)) * q_decay
        kd = (k.astype(F32) * k_decay).astype(BF16)
        kv = lax.dot_general(kd, v, (((0,), (0,)), ((), ())), preferred_element_type=F32)
        state_ref[...] = state * chunk_decay + kv
        mu = jnp.mean(o, axis=-1, keepdims=True)
        oc = o - mu
        var = jnp.mean(oc * oc, axis=-1, keepdims=True)
        on = oc * lax.rsqrt(var + EPS)
        o_ref[sl, :] = (on * _silu(g_ref[sl, :].astype(F32))).astype(o_ref.dtype)


def _retention(p_ret, log_g):
    m = p_ret.shape[0]

    def spec(off):
        return pl.BlockSpec((RET_T, RET_DIM), lambda h, s: (s, off + h))

    return pl.pallas_call(
        _ret_kernel,
        grid=(RET_HEADS, m // RET_T),
        in_specs=[pl.BlockSpec(memory_space=pltpu.SMEM),
                  spec(0), spec(RET_HEADS), spec(2 * RET_HEADS), spec(3 * RET_HEADS)],
        out_specs=pl.BlockSpec((RET_T, RET_DIM), lambda h, s: (s, h)),
        out_shape=jax.ShapeDtypeStruct((m, RET_W), BF16),
        scratch_shapes=[pltpu.VMEM((RET_DIM, RET_DIM), F32)],
        compiler_params=_params("arbitrary", "arbitrary"),
        name="retention",
    )(log_g, p_ret, p_ret, p_ret, p_ret)


def _moba_kernel(sl_ref, q_ref, k_ref, v_ref, e_ref, o_ref, kb_ref, vb_ref, km_ref):
    hd = pl.program_id(0)
    i = pl.program_id(1)
    bs = MOBA_BLOCK

    @pl.when(i == 0)
    def _():
        kf = k_ref[...]
        km_ref[...] = jnp.mean(kf.reshape(N_BLOCKS, bs, MOBA_DIM), axis=1)
        kb_ref[...] = kf.astype(BF16)
        vb_ref[...] = v_ref[...].astype(BF16)

    slope = sl_ref[hd]
    scale = MOBA_DIM ** -0.5
    q = q_ref[...]
    gate = lax.dot_general(q, km_ref[...], (((1,), (1,)), ((), ())),
                           precision=lax.Precision.HIGHEST, preferred_element_type=F32)
    col = lax.broadcasted_iota(jnp.int32, gate.shape, 1)
    g = jnp.where(col < i, gate, -jnp.inf)
    selneg = jnp.full(gate.shape, NEG_BIG, F32)
    for _ in range(MOBA_TOPK):
        mx = jnp.max(g, axis=1, keepdims=True)
        idx = jnp.min(jnp.where(g == mx, col, 2 * N_BLOCKS), axis=1, keepdims=True)
        hit = col == idx
        selneg = jnp.where(jnp.logical_and(hit, mx > -jnp.inf), 0.0, selneg)
        g = jnp.where(hit, -jnp.inf, g)
    selneg = selneg.astype(BF16)
    qb = q.astype(BF16)

    def scores(n_start, kn):
        s = lax.dot_general(qb, kn, (((1,), (1,)), ((), ())), preferred_element_type=F32) * scale
        cpos = lax.broadcasted_iota(jnp.int32, (1, bs), 1) + (n_start - i * bs)
        return s + slope * cpos.astype(F32)

    q0 = pl.multiple_of(i * bs, bs)
    s = scores(q0, kb_ref[pl.ds(q0, bs), :])
    ri = lax.broadcasted_iota(jnp.int32, (bs, bs), 0)
    ci = lax.broadcasted_iota(jnp.int32, (bs, bs), 1)
    s = jnp.where(ci <= ri, s, NEG_BIG)
    m0 = jnp.max(s, axis=1, keepdims=True)
    p = jnp.exp(s - m0)
    l0 = jnp.sum(p, axis=1, keepdims=True)
    acc0 = _dot(p.astype(BF16), vb_ref[pl.ds(q0, bs), :])

    def body(n, carry):
        m, l, acc = carry
        r = pl.multiple_of(n * bs, bs)
        s = scores(r, kb_ref[pl.ds(r, bs), :]) + _dot(selneg, e_ref[n])
        m_new = jnp.maximum(m, jnp.max(s, axis=1, keepdims=True))
        alpha = jnp.exp(m - m_new)
        p = jnp.exp(s - m_new)
        l = alpha * l + jnp.sum(p, axis=1, keepdims=True)
        acc = alpha * acc + _dot(p.astype(BF16), vb_ref[pl.ds(r, bs), :])
        return m_new, l, acc

    _, l, acc = lax.fori_loop(0, i, body, (m0, l0, acc0))
    o_ref[...] = (acc / l).astype(o_ref.dtype)


def _moba(p_moba, slopes, expand):
    m = p_moba.shape[0]
    h = MOBA_HEADS
    return pl.pallas_call(
        _moba_kernel,
        grid=(h, m // MOBA_BLOCK),
        in_specs=[
            pl.BlockSpec(memory_space=pltpu.SMEM),
            pl.BlockSpec((MOBA_BLOCK, MOBA_DIM), lambda hd, i: (i, hd)),
            pl.BlockSpec((m, MOBA_DIM), lambda hd, i: (0, h + hd)),
            pl.BlockSpec((m, MOBA_DIM), lambda hd, i: (0, 2 * h + hd)),
            pl.BlockSpec((N_BLOCKS, N_BLOCKS, MOBA_BLOCK), lambda hd, i: (0, 0, 0)),
        ],
        out_specs=pl.BlockSpec((MOBA_BLOCK, MOBA_DIM), lambda hd, i: (i, hd)),
        out_shape=jax.ShapeDtypeStruct((m, MOBA_W), BF16),
        scratch_shapes=[
            pltpu.VMEM((m, MOBA_DIM), BF16),
            pltpu.VMEM((m, MOBA_DIM), BF16),
            pltpu.VMEM((N_BLOCKS, MOBA_DIM), F32),
        ],
        compiler_params=_params("arbitrary", "arbitrary"),
        name="moba",
    )(slopes, p_moba, p_moba, p_moba, expand)


def _split_mod(mod):
    d = D_MODEL
    return mod[:, :d], mod[:, d:2 * d], mod[:, 2 * d:]


def _ffn(x, c_col, norm_g, mod_w, mod_b, w1, w3, w2):
    shift, scale, gate = _split_mod(_mod_vector(c_col, mod_w, mod_b))
    h = _norm_modulate(x, norm_g, scale, shift)
    u = _swiglu_up(h, w1.astype(BF16), w3.astype(BF16))
    return _project_residual(u, w2.astype(BF16), x, gate, tk=FFN_HIDDEN // 2, name="ffn_down")


def kernel(x, c, l0_mix_norm_g, l0_mix_mod_w, l0_mix_mod_b, l0_mix_w_in, l0_mix_w_out, l0_ffn_norm_g, l0_ffn_mod_w, l0_ffn_mod_b, l0_ffn_w1, l0_ffn_w3, l0_ffn_w2, l1_mix_norm_g, l1_mix_mod_w, l1_mix_mod_b, l1_mix_w_in, l1_mix_conv_w, l1_mix_w_out, l1_ffn_norm_g, l1_ffn_mod_w, l1_ffn_mod_b, l1_ffn_w1, l1_ffn_w3, l1_ffn_w2, final_norm_g):
    b, seq, d = x.shape
    xs = x.reshape(b * seq, d)
    c_col = c.reshape(d, 1)

    log_g = jnp.log1p(-jnp.exp2(-5.0 - jnp.arange(RET_HEADS, dtype=F32)))
    slopes = jnp.exp2(-8.0 * jnp.arange(1, MOBA_HEADS + 1, dtype=F32) / MOBA_HEADS)
    blk = jnp.arange(N_BLOCKS)
    expand = jnp.broadcast_to(
        (blk[:, None] == blk[None, :])[:, :, None], (N_BLOCKS, N_BLOCKS, MOBA_BLOCK)).astype(BF16)

    shift, scale, gate = _split_mod(_mod_vector(c_col, l0_mix_mod_w, l0_mix_mod_b))
    h = _norm_modulate(xs, l0_mix_norm_g, scale, shift)
    w_in = l0_mix_w_in.astype(BF16)
    p_ret = _project(h, w_in, 0, 4 * RET_W, BF16, name="mix0_in_ret")
    p_moba = _project(h, w_in, 4 * RET_W, 3 * MOBA_W, F32, name="mix0_in_moba")
    ret = _retention(p_ret, log_g)
    att = _moba(p_moba, slopes, expand)
    mix = jnp.concatenate([ret, att], axis=-1)
    xs = _project_residual(mix, l0_mix_w_out.astype(BF16), xs, gate, name="mix0_out")
    xs = _ffn(xs, c_col, l0_ffn_norm_g, l0_ffn_mod_w, l0_ffn_mod_b, l0_ffn_w1, l0_ffn_w3, l0_ffn_w2)

    shift, scale, gate = _split_mod(_mod_vector(c_col, l1_mix_mod_w, l1_mix_mod_b))
    h = _norm_modulate(xs, l1_mix_norm_g, scale, shift)
    s = _conv_project(h, l1_mix_w_in.astype(BF16), l1_mix_conv_w.reshape(3, d))
    xs = _project_residual(s, l1_mix_w_out.astype(BF16), xs, gate, name="mix1_out")
    xs = _ffn(xs, c_col, l1_ffn_norm_g, l1_ffn_mod_w, l1_ffn_mod_b, l1_ffn_w1, l1_ffn_w3, l1_ffn_w2)

    return _final_norm(xs, final_norm_g).reshape(b, seq, d)
```

```python
import functools

import jax
import jax.numpy as jnp
from jax import lax
from jax.experimental import pallas as pl
from jax.experimental.pallas import tpu as pltpu

D_MODEL = 4096
SEQ = 8192
RET_HEADS = 8
RET_DIM = 256
RET_CHUNK = 128
MOBA_HEADS = 16
MOBA_DIM = 128
MOBA_BLOCK = 256
MOBA_TOPK = 3
FFN_HIDDEN = 11008
EPS = 1e-6
RET_W = RET_HEADS * RET_DIM
MOBA_W = MOBA_HEADS * MOBA_DIM
N_BLOCKS = SEQ // MOBA_BLOCK

VMEM_LIMIT_BYTES = 56 * 1024 * 1024
NEG_BIG = -1e30

BF16 = jnp.bfloat16
F32 = jnp.float32


def _params(*sem):
    return pltpu.CompilerParams(dimension_semantics=sem, vmem_limit_bytes=VMEM_LIMIT_BYTES)


def _silu(v):
    return v * jax.nn.sigmoid(v)


MOD_TN = 512
MOD_KC = 128


def _mod_kernel(c_ref, w_ref, b_ref, o_ref, ca_ref):
    lanes = ca_ref.shape[1]

    @pl.when(pl.program_id(0) == 0)
    def _():
        ca_ref[...] = jnp.broadcast_to(_silu(c_ref[...]), ca_ref.shape)

    def body(k, acc):
        r = pl.multiple_of(k * MOD_KC, MOD_KC)
        ca = ca_ref[pl.ds(r, MOD_KC), :]
        p = w_ref[pl.ds(r, MOD_KC), :] * jnp.concatenate([ca] * (MOD_TN // lanes), axis=1)
        return acc + jnp.sum(p.reshape(MOD_KC // 8, 8, MOD_TN), axis=0)

    acc = lax.fori_loop(0, D_MODEL // MOD_KC, body, jnp.zeros((8, MOD_TN), F32), unroll=4)
    o_ref[...] = jnp.sum(acc, axis=0, keepdims=True) + b_ref[...]


def _mod_vector(c_col, w, b):
    n = w.shape[1]
    return pl.pallas_call(
        _mod_kernel,
        grid=(n // MOD_TN,),
        in_specs=[
            pl.BlockSpec((D_MODEL, 1), lambda j: (0, 0)),
            pl.BlockSpec((D_MODEL, MOD_TN), lambda j: (0, j)),
            pl.BlockSpec((1, MOD_TN), lambda j: (0, j)),
        ],
        out_specs=pl.BlockSpec((1, MOD_TN), lambda j: (0, j)),
        out_shape=jax.ShapeDtypeStruct((1, n), F32),
        scratch_shapes=[pltpu.VMEM((D_MODEL, 128), F32)],
        compiler_params=_params("arbitrary"),
        name="adaln_mod",
    )(c_col, w, b.reshape(1, n))


NORM_TM = 256


NORM_RC = 16


def _norm_rows(x_ref, o_ref, mult, add):
    def body(r, carry):
        rows = pl.ds(pl.multiple_of(r * NORM_RC, NORM_RC), NORM_RC)
        x = x_ref[rows, :]
        ms = jnp.mean(x * x, axis=-1, keepdims=True)
        y = x * lax.rsqrt(ms + EPS) * mult
        if add is not None:
            y = y + add
        o_ref[rows, :] = y.astype(o_ref.dtype)
        return carry

    lax.fori_loop(0, x_ref.shape[0] // NORM_RC, body, 0, unroll=8)


def _normmod_kernel(x_ref, g_ref, sc_ref, sh_ref, o_ref):
    _norm_rows(x_ref, o_ref, g_ref[...] * (1.0 + sc_ref[...]), sh_ref[...])


def _norm_kernel(x_ref, g_ref, o_ref):
    _norm_rows(x_ref, o_ref, g_ref[...], None)


def _norm_modulate(x, g, scale, shift):
    m, d = x.shape
    row = pl.BlockSpec((NORM_TM, d), lambda i: (i, 0))
    vec = pl.BlockSpec((1, d), lambda i: (0, 0))
    return pl.pallas_call(
        _normmod_kernel,
        grid=(m // NORM_TM,),
        in_specs=[row, vec, vec, vec],
        out_specs=row,
        out_shape=jax.ShapeDtypeStruct((m, d), BF16),
        compiler_params=_params("arbitrary"),
        name="norm_modulate",
    )(x, g.reshape(1, d), scale, shift)


def _final_norm(x, g):
    m, d = x.shape
    row = pl.BlockSpec((NORM_TM, d), lambda i: (i, 0))
    vec = pl.BlockSpec((1, d), lambda i: (0, 0))
    return pl.pallas_call(
        _norm_kernel,
        grid=(m // NORM_TM,),
        in_specs=[row, vec],
        out_specs=row,
        out_shape=jax.ShapeDtypeStruct((m, d), F32),
        compiler_params=_params("arbitrary"),
        name="final_norm",
    )(x, g.reshape(1, d))


def _dot(a, b):
    return jnp.dot(a, b.astype(BF16), preferred_element_type=F32)


def _proj_kernel(a_ref, w_ref, o_ref):
    o_ref[...] = _dot(a_ref[...], w_ref[...]).astype(o_ref.dtype)


def _project(a, w, col0, ncols, out_dtype, tm=1024, tn=512, name="proj"):
    m, k = a.shape
    cb = col0 // tn
    return pl.pallas_call(
        _proj_kernel,
        grid=(m // tm, ncols // tn),
        in_specs=[
            pl.BlockSpec((tm, k), lambda i, j: (i, 0)),
            pl.BlockSpec((k, tn), lambda i, j: (0, cb + j)),
        ],
        out_specs=pl.BlockSpec((tm, tn), lambda i, j: (i, j)),
        out_shape=jax.ShapeDtypeStruct((m, ncols), out_dtype),
        compiler_params=_params("arbitrary", "arbitrary"),
        name=name,
    )(a, w)


def _swiglu_kernel(a_ref, w1_ref, w3_ref, o_ref):
    a = a_ref[...]
    h1 = _dot(a, w1_ref[...])
    h3 = _dot(a, w3_ref[...])
    o_ref[...] = (_silu(h1) * h3).astype(o_ref.dtype)


def _swiglu_up(a, w1, w3, tm=1024, tn=256):
    m, k = a.shape
    n = w1.shape[1]
    wspec = pl.BlockSpec((k, tn), lambda i, j: (0, j))
    return pl.pallas_call(
        _swiglu_kernel,
        grid=(m // tm, n // tn),
        in_specs=[pl.BlockSpec((tm, k), lambda i, j: (i, 0)), wspec, wspec],
        out_specs=pl.BlockSpec((tm, tn), lambda i, j: (i, j)),
        out_shape=jax.ShapeDtypeStruct((m, n), BF16),
        compiler_params=_params("arbitrary", "arbitrary"),
        name="swiglu_up",
    )(a, w1, w3)


def _resid_kernel(a_ref, w_ref, x_ref, g_ref, o_ref, acc_ref, *, nk):
    part = _dot(a_ref[...], w_ref[...])
    if nk == 1:
        o_ref[...] = x_ref[...] + g_ref[...] * part
        return
    k = pl.program_id(2)

    @pl.when(k == 0)
    def _():
        acc_ref[...] = part

    @pl.when(jnp.logical_and(k > 0, k < nk - 1))
    def _():
        acc_ref[...] += part

    @pl.when(k == nk - 1)
    def _():
        o_ref[...] = x_ref[...] + g_ref[...] * (acc_ref[...] + part)


def _project_residual(a, w, x, gate, tm=1024, tn=512, tk=None, name="proj_resid"):
    m, k = a.shape
    n = w.shape[1]
    tk = k if tk is None else tk
    nk = k // tk
    return pl.pallas_call(
        functools.partial(_resid_kernel, nk=nk),
        grid=(m // tm, n // tn, nk),
        in_specs=[
            pl.BlockSpec((tm, tk), lambda i, j, kk: (i, kk)),
            pl.BlockSpec((tk, tn), lambda i, j, kk: (kk, j)),
            pl.BlockSpec((tm, tn), lambda i, j, kk: (i, j)),
            pl.BlockSpec((1, tn), lambda i, j, kk: (0, j)),
        ],
        out_specs=pl.BlockSpec((tm, tn), lambda i, j, kk: (i, j)),
        out_shape=jax.ShapeDtypeStruct((m, n), F32),
        scratch_shapes=[pltpu.VMEM((tm, tn), F32)],
        compiler_params=_params("arbitrary", "arbitrary", "arbitrary"),
        name=name,
    )(a, w, x, gate)


CONV_HALO = 16


def _conv_proj_kernel(a_ref, ap_ref, wb_ref, wc_ref, wu_ref, cw_ref, o_ref):
    i = pl.program_id(0)
    a = a_ref[...]
    b = _dot(a, wb_ref[...])
    z = _dot(a, wc_ref[...]) * _dot(a, wu_ref[...])
    ap = ap_ref[...]
    zp = _dot(ap, wc_ref[...]) * _dot(ap, wu_ref[...])
    zp = jnp.where(i > 0, zp, 0.0)
    p1 = zp[CONV_HALO - 1:CONV_HALO, :]
    p2 = zp[CONV_HALO - 2:CONV_HALO - 1, :]
    row = lax.broadcasted_iota(jnp.int32, z.shape, 0)
    z1 = jnp.where(row == 0, p1, pltpu.roll(z, 1, 0))
    z2 = jnp.where(row == 0, p2, jnp.where(row == 1, p1, pltpu.roll(z, 2, 0)))
    cw = cw_ref[...]
    y = cw[2:3, :] * z + cw[1:2, :] * z1 + cw[0:1, :] * z2
    o_ref[...] = (b * y).astype(o_ref.dtype)


def _conv_project(a, w, conv_w, tm=1024, tn=256):
    m, k = a.shape
    n = D_MODEL
    nb = n // tn
    hb = tm // CONV_HALO
    return pl.pallas_call(
        _conv_proj_kernel,
        grid=(m // tm, nb),
        in_specs=[
            pl.BlockSpec((tm, k), lambda i, j: (i, 0)),
            pl.BlockSpec((CONV_HALO, k), lambda i, j: (jnp.maximum(i * hb - 1, 0), 0)),
            pl.BlockSpec((k, tn), lambda i, j: (0, j)),
            pl.BlockSpec((k, tn), lambda i, j: (0, nb + j)),
            pl.BlockSpec((k, tn), lambda i, j: (0, 2 * nb + j)),
            pl.BlockSpec((3, tn), lambda i, j: (0, j)),
        ],
        out_specs=pl.BlockSpec((tm, tn), lambda i, j: (i, j)),
        out_shape=jax.ShapeDtypeStruct((m, n), BF16),
        compiler_params=_params("arbitrary", "arbitrary"),
        name="conv_proj",
    )(a, a, w, w, w, conv_w)


RET_T = 512


def _ret_kernel(lg_ref, q_ref, k_ref, v_ref, g_ref, o_ref, state_ref):
    h = pl.program_id(0)

    @pl.when(pl.program_id(1) == 0)
    def _():
        state_ref[...] = jnp.zeros_like(state_ref)

    c = RET_CHUNK
    lg = lg_ref[h]
    inv_scale = RET_DIM ** -0.5
    ri = lax.broadcasted_iota(jnp.int32, (c, c), 0)
    ci = lax.broadcasted_iota(jnp.int32, (c, c), 1)
    rel = (ri - ci).astype(F32)
    intra = jnp.where(rel >= 0, jnp.exp(lg * jnp.maximum(rel, 0.0)), 0.0) * inv_scale
    pos = lax.broadcasted_iota(jnp.int32, (c, 1), 0).astype(F32)
    q_decay = jnp.exp(lg * (pos + 1.0))
    k_decay = jnp.exp(lg * (c - 1.0 - pos)) * inv_scale
    chunk_decay = jnp.exp(jnp.full((1, 1), lg * c, F32))

    for t in range(RET_T // c):
        sl = slice(t * c, (t + 1) * c)
        q = q_ref[sl, :]
        k = k_ref[sl, :]
        v = v_ref[sl, :]
        state = state_ref[...]
        scores = lax.dot_general(q, k, (((1,), (1,)), ((), ())), preferred_element_type=F32) * intra
        o = _dot(scores.astype(BF16), v)
        o = o + _dot(q, state.astype(BF16)) * q_decay
        kd = (k.astype(F32) * k_decay).astype(BF16)
        kv = lax.dot_general(kd, v, (((0,), (0,)), ((), ())), preferred_element_type=F32)
        state_ref[...] = state * chunk_decay + kv
        mu = jnp.mean(o, axis=-1, keepdims=True)
        oc = o - mu
        var = jnp.mean(oc * oc, axis=-1, keepdims=True)
        on = oc * lax.rsqrt(var + EPS)
        o_ref[sl, :] = (on * _silu(g_ref[sl, :].astype(F32))).astype(o_ref.dtype)


def _retention(p_ret, log_g):
    m = p_ret.shape[0]

    def spec(off):
        return pl.BlockSpec((RET_T, RET_DIM), lambda h, s: (s, off + h))

    return pl.pallas_call(
        _ret_kernel,
        grid=(RET_HEADS, m // RET_T),
        in_specs=[pl.BlockSpec(memory_space=pltpu.SMEM),
                  spec(0), spec(RET_HEADS), spec(2 * RET_HEADS), spec(3 * RET_HEADS)],
        out_specs=pl.BlockSpec((RET_T, RET_DIM), lambda h, s: (s, h)),
        out_shape=jax.ShapeDtypeStruct((m, RET_W + MOBA_W), BF16),
        scratch_shapes=[pltpu.VMEM((RET_DIM, RET_DIM), F32)],
        compiler_params=_params("arbitrary", "arbitrary"),
        name="retention",
    )(log_g, p_ret, p_ret, p_ret, p_ret)


MOBA_KC = 1024
MOBA_FILL = 2048
MOBA_HP = 2
MOBA_BLOCK_SHIFT = MOBA_BLOCK.bit_length() - 1


def _moba_kernel(sl_ref, q_ref, k_ref, v_ref, mix_ref, o_ref, ka_ref, va_ref, km_ref, s_ref, acc_ref):
    del mix_ref
    i = pl.program_id(1)
    bs = MOBA_BLOCK
    d = MOBA_DIM
    kc = MOBA_KC
    hp = MOBA_HP
    seq = k_ref.shape[0]
    heads = range(hp)

    @pl.when(i == 0)
    def _():
        km_ref[...] = jnp.zeros_like(km_ref)
        fr = MOBA_FILL
        lane = lax.broadcasted_iota(jnp.int32, (fr, d), 1)

        def fill(c, carry):
            r = pl.multiple_of(c * fr, fr)
            blk = jnp.right_shift(lax.broadcasted_iota(jnp.int32, (fr, d), 0) + r, MOBA_BLOCK_SHIFT)
            onehot = jnp.where(lane == blk, 1.0, 0.0).astype(BF16)
            ones_col = jnp.where(lane == 0, 1.0, 0.0).astype(BF16)
            for hh in heads:
                kb = k_ref[pl.ds(r, fr), hh * d:(hh + 1) * d]
                ka_ref[hh, pl.ds(r, fr), 0:d] = kb
                ka_ref[hh, pl.ds(r, fr), d:2 * d] = onehot
                va_ref[hh, pl.ds(r, fr), 0:d] = v_ref[pl.ds(r, fr), hh * d:(hh + 1) * d]
                va_ref[hh, pl.ds(r, fr), d:2 * d] = ones_col
                kmean = jnp.mean(kb.astype(F32).reshape(fr // bs, bs, d), axis=1)
                km_ref[hh, pl.ds(pl.multiple_of(c * (fr // bs), fr // bs), fr // bs), :] = kmean
            return carry

        lax.fori_loop(0, seq // fr, fill, 0)

    scale = d ** -0.5
    slopes = [sl_ref[pl.program_id(0) * hp + hh] for hh in heads]

    q_augs = []
    for hh in heads:
        q = q_ref[:, hh * d:(hh + 1) * d]
        gate = lax.dot_general(q, km_ref[hh], (((1,), (1,)), ((), ())),
                               precision=lax.Precision.HIGHEST, preferred_element_type=F32)
        col = lax.broadcasted_iota(jnp.int32, gate.shape, 1)
        g = jnp.where(col < i, gate, -jnp.inf)
        bias = jnp.where(col == i, 0.0, NEG_BIG)
        for _ in range(MOBA_TOPK):
            mx = jnp.max(g, axis=1, keepdims=True)
            idx = jnp.min(jnp.where(g == mx, col, 2 * d), axis=1, keepdims=True)
            hit = col == idx
            bias = jnp.where(jnp.logical_and(hit, mx > -jnp.inf), 0.0, bias)
            g = jnp.where(hit, -jnp.inf, g)
        q_augs.append(jnp.concatenate([q.astype(BF16), bias.astype(BF16)], axis=1))

    def scores(j, causal, hh):
        r = pl.multiple_of(j * kc, kc)
        s = lax.dot_general(q_augs[hh], ka_ref[hh, pl.ds(r, kc), :], (((1,), (1,)), ((), ())),
                            preferred_element_type=F32) * scale
        rel = lax.broadcasted_iota(jnp.int32, (1, kc), 1) + (r - i * bs)
        s = s + slopes[hh] * rel.astype(F32)
        if causal:
            row = lax.broadcasted_iota(jnp.int32, (bs, kc), 0)
            s = jnp.where(rel <= row, s, NEG_BIG)
        s_ref[hh, j] = s
        part = s[:, 0:d]
        for t in range(1, kc // d):
            part = jnp.maximum(part, s[:, t * d:(t + 1) * d])
        return part

    def sweep(j, mxs, causal):
        return tuple(jnp.maximum(mxs[hh], scores(j, causal, hh)) for hh in heads)

    n_past = i // (kc // bs)
    mxs = tuple(jnp.full((bs, d), -jnp.inf, F32) for _ in heads)
    mxs = lax.fori_loop(0, n_past, lambda j, c: sweep(j, c, False), mxs)
    mxs = sweep(n_past, mxs, True)
    ms = [jnp.max(mxs[hh], axis=1, keepdims=True) for hh in heads]

    acc_ref[...] = jnp.zeros_like(acc_ref)

    def weighted(j, carry):
        r = pl.multiple_of(j * kc, kc)
        for hh in heads:
            p = jnp.exp(s_ref[hh, j] - ms[hh]).astype(BF16)
            acc_ref[hh] += _dot(p, va_ref[hh, pl.ds(r, kc), :])
        return carry

    lax.fori_loop(0, n_past + 1, weighted, 0)
    for hh in heads:
        acc = acc_ref[hh]
        o_ref[:, hh * d:(hh + 1) * d] = (acc[:, 0:d] / acc[:, d:d + 1]).astype(o_ref.dtype)


def _moba(q_all, kv_all, slopes, mix):
    m = q_all.shape[0]
    hp = MOBA_HP
    w = hp * MOBA_DIM
    groups = q_all.shape[1] // w
    staged = pltpu.VMEM((hp, m, 2 * MOBA_DIM), BF16)
    return pl.pallas_call(
        _moba_kernel,
        grid=(groups, m // MOBA_BLOCK),
        in_specs=[
            pl.BlockSpec(memory_space=pltpu.SMEM),
            pl.BlockSpec((MOBA_BLOCK, w), lambda g, i: (i, g)),
            pl.BlockSpec((m, w), lambda g, i: (0, g)),
            pl.BlockSpec((m, w), lambda g, i: (0, groups + g)),
            pl.BlockSpec(memory_space=pl.ANY),
        ],
        out_specs=pl.BlockSpec((MOBA_BLOCK, w), lambda g, i: (i, RET_W // w + g)),
        out_shape=jax.ShapeDtypeStruct(mix.shape, BF16),
        input_output_aliases={4: 0},
        scratch_shapes=[
            staged,
            staged,
            pltpu.VMEM((hp, MOBA_DIM, MOBA_DIM), F32),
            pltpu.VMEM((hp, m // MOBA_KC, MOBA_BLOCK, MOBA_KC), F32),
            pltpu.VMEM((hp, MOBA_BLOCK, 2 * MOBA_DIM), F32),
        ],
        compiler_params=_params("arbitrary", "arbitrary"),
        name="moba",
    )(slopes, q_all, kv_all, kv_all, mix)


def _split_mod(mod):
    d = D_MODEL
    return mod[:, :d], mod[:, d:2 * d], mod[:, 2 * d:]


def _ffn(x, c_col, norm_g, mod_w, mod_b, w1, w3, w2):
    shift, scale, gate = _split_mod(_mod_vector(c_col, mod_w, mod_b))
    h = _norm_modulate(x, norm_g, scale, shift)
    u = _swiglu_up(h, w1, w3)
    return _project_residual(u, w2.astype(BF16), x, gate, tk=FFN_HIDDEN // 2, name="ffn_down")


def kernel(x, c, l0_mix_norm_g, l0_mix_mod_w, l0_mix_mod_b, l0_mix_w_in, l0_mix_w_out, l0_ffn_norm_g, l0_ffn_mod_w, l0_ffn_mod_b, l0_ffn_w1, l0_ffn_w3, l0_ffn_w2, l1_mix_norm_g, l1_mix_mod_w, l1_mix_mod_b, l1_mix_w_in, l1_mix_conv_w, l1_mix_w_out, l1_ffn_norm_g, l1_ffn_mod_w, l1_ffn_mod_b, l1_ffn_w1, l1_ffn_w3, l1_ffn_w2, final_norm_g):
    b, seq, d = x.shape
    xs = x.reshape(b * seq, d)
    c_col = c.reshape(d, 1)

    log_g = jnp.log1p(-jnp.exp2(-5.0 - jnp.arange(RET_HEADS, dtype=F32)))
    slopes = jnp.exp2(-8.0 * jnp.arange(1, MOBA_HEADS + 1, dtype=F32) / MOBA_HEADS)

    shift, scale, gate = _split_mod(_mod_vector(c_col, l0_mix_mod_w, l0_mix_mod_b))
    h = _norm_modulate(xs, l0_mix_norm_g, scale, shift)
    w_in = l0_mix_w_in
    p_ret = _project(h, w_in, 0, 4 * RET_W, BF16, name="mix0_in_ret")
    moba_q = _project(h, w_in, 4 * RET_W, MOBA_W, F32, name="mix0_in_q")
    moba_kv = _project(h, w_in, 4 * RET_W + MOBA_W, 2 * MOBA_W, BF16, name="mix0_in_kv")
    mix = _moba(moba_q, moba_kv, slopes, _retention(p_ret, log_g))
    xs = _project_residual(mix, l0_mix_w_out, xs, gate, name="mix0_out")
    xs = _ffn(xs, c_col, l0_ffn_norm_g, l0_ffn_mod_w, l0_ffn_mod_b, l0_ffn_w1, l0_ffn_w3, l0_ffn_w2)

    shift, scale, gate = _split_mod(_mod_vector(c_col, l1_mix_mod_w, l1_mix_mod_b))
    h = _norm_modulate(xs, l1_mix_norm_g, scale, shift)
    s = _conv_project(h, l1_mix_w_in.astype(BF16), l1_mix_conv_w.reshape(3, d))
    xs = _project_residual(s, l1_mix_w_out, xs, gate, name="mix1_out")
    xs = _ffn(xs, c_col, l1_ffn_norm_g, l1_ffn_mod_w, l1_ffn_mod_b, l1_ffn_w1, l1_ffn_w3, l1_ffn_w2)

    return _final_norm(xs, final_norm_g).reshape(b, seq, d)
```

```python
import functools

import jax
import jax.numpy as jnp
from jax import lax
from jax.experimental import pallas as pl
from jax.experimental.pallas import tpu as pltpu

D_MODEL = 4096
SEQ = 8192
RET_HEADS = 8
RET_DIM = 256
RET_CHUNK = 128
MOBA_HEADS = 16
MOBA_DIM = 128
MOBA_BLOCK = 256
MOBA_TOPK = 3
FFN_HIDDEN = 11008
EPS = 1e-6
RET_W = RET_HEADS * RET_DIM
MOBA_W = MOBA_HEADS * MOBA_DIM
N_BLOCKS = SEQ // MOBA_BLOCK

VMEM_LIMIT_BYTES = 56 * 1024 * 1024
NEG_BIG = -1e30

BF16 = jnp.bfloat16
F32 = jnp.float32


def _params(*sem):
    return pltpu.CompilerParams(dimension_semantics=sem, vmem_limit_bytes=VMEM_LIMIT_BYTES)


def _silu(v):
    return v * jax.nn.sigmoid(v)


MOD_TN = 512
MOD_KC = 128


def _mod_kernel(c_ref, w_ref, b_ref, o_ref, ca_ref):
    lanes = ca_ref.shape[1]

    @pl.when(pl.program_id(0) == 0)
    def _():
        ca_ref[...] = jnp.broadcast_to(_silu(c_ref[...]), ca_ref.shape)

    def body(k, acc):
        r = pl.multiple_of(k * MOD_KC, MOD_KC)
        ca = ca_ref[pl.ds(r, MOD_KC), :]
        p = w_ref[pl.ds(r, MOD_KC), :] * jnp.concatenate([ca] * (MOD_TN // lanes), axis=1)
        return acc + jnp.sum(p.reshape(MOD_KC // 8, 8, MOD_TN), axis=0)

    acc = lax.fori_loop(0, D_MODEL // MOD_KC, body, jnp.zeros((8, MOD_TN), F32), unroll=4)
    o_ref[...] = jnp.sum(acc, axis=0, keepdims=True) + b_ref[...]


def _mod_vector(c_col, w, b):
    n = w.shape[1]
    return pl.pallas_call(
        _mod_kernel,
        grid=(n // MOD_TN,),
        in_specs=[
            pl.BlockSpec((D_MODEL, 1), lambda j: (0, 0)),
            pl.BlockSpec((D_MODEL, MOD_TN), lambda j: (0, j)),
            pl.BlockSpec((1, MOD_TN), lambda j: (0, j)),
        ],
        out_specs=pl.BlockSpec((1, MOD_TN), lambda j: (0, j)),
        out_shape=jax.ShapeDtypeStruct((1, n), F32),
        scratch_shapes=[pltpu.VMEM((D_MODEL, 128), F32)],
        compiler_params=_params("arbitrary"),
        name="adaln_mod",
    )(c_col, w, b.reshape(1, n))


NORM_TM = 256


NORM_RC = 16


def _norm_rows(x_ref, o_ref, mult, add):
    def body(r, carry):
        rows = pl.ds(pl.multiple_of(r * NORM_RC, NORM_RC), NORM_RC)
        x = x_ref[rows, :]
        ms = jnp.mean(x * x, axis=-1, keepdims=True)
        y = x * lax.rsqrt(ms + EPS) * mult
        if add is not None:
            y = y + add
        o_ref[rows, :] = y.astype(o_ref.dtype)
        return carry

    lax.fori_loop(0, x_ref.shape[0] // NORM_RC, body, 0, unroll=8)


def _normmod_kernel(x_ref, g_ref, sc_ref, sh_ref, o_ref):
    _norm_rows(x_ref, o_ref, g_ref[...] * (1.0 + sc_ref[...]), sh_ref[...])


def _norm_kernel(x_ref, g_ref, o_ref):
    _norm_rows(x_ref, o_ref, g_ref[...], None)


def _norm_modulate(x, g, scale, shift):
    m, d = x.shape
    row = pl.BlockSpec((NORM_TM, d), lambda i: (i, 0))
    vec = pl.BlockSpec((1, d), lambda i: (0, 0))
    return pl.pallas_call(
        _normmod_kernel,
        grid=(m // NORM_TM,),
        in_specs=[row, vec, vec, vec],
        out_specs=row,
        out_shape=jax.ShapeDtypeStruct((m, d), BF16),
        compiler_params=_params("arbitrary"),
        name="norm_modulate",
    )(x, g.reshape(1, d), scale, shift)


def _final_norm(x, g):
    m, d = x.shape
    row = pl.BlockSpec((NORM_TM, d), lambda i: (i, 0))
    vec = pl.BlockSpec((1, d), lambda i: (0, 0))
    return pl.pallas_call(
        _norm_kernel,
        grid=(m // NORM_TM,),
        in_specs=[row, vec],
        out_specs=row,
        out_shape=jax.ShapeDtypeStruct((m, d), F32),
        compiler_params=_params("arbitrary"),
        name="final_norm",
    )(x, g.reshape(1, d))


def _dot(a, b):
    return jnp.dot(a, b.astype(BF16), preferred_element_type=F32)


def _proj_kernel(a_ref, w_ref, o_ref):
    o_ref[...] = _dot(a_ref[...], w_ref[...]).astype(o_ref.dtype)


def _project(a, w, col0, ncols, out_dtype, tm=1024, tn=512, name="proj"):
    m, k = a.shape
    cb = col0 // tn
    return pl.pallas_call(
        _proj_kernel,
        grid=(m // tm, ncols // tn),
        in_specs=[
            pl.BlockSpec((tm, k), lambda i, j: (i, 0)),
            pl.BlockSpec((k, tn), lambda i, j: (0, cb + j)),
        ],
        out_specs=pl.BlockSpec((tm, tn), lambda i, j: (i, j)),
        out_shape=jax.ShapeDtypeStruct((m, ncols), out_dtype),
        compiler_params=_params("arbitrary", "arbitrary"),
        name=name,
    )(a, w)


def _swiglu_kernel(a_ref, w1_ref, w3_ref, o_ref):
    a = a_ref[...]
    h1 = _dot(a, w1_ref[...])
    h3 = _dot(a, w3_ref[...])
    o_ref[...] = (_silu(h1) * h3).astype(o_ref.dtype)


def _swiglu_up(a, w1, w3, tm=2048, tn=256):
    m, k = a.shape
    n = w1.shape[1]
    wspec = pl.BlockSpec((k, tn), lambda i, j: (0, j))
    return pl.pallas_call(
        _swiglu_kernel,
        grid=(m // tm, n // tn),
        in_specs=[pl.BlockSpec((tm, k), lambda i, j: (i, 0)), wspec, wspec],
        out_specs=pl.BlockSpec((tm, tn), lambda i, j: (i, j)),
        out_shape=jax.ShapeDtypeStruct((m, n), BF16),
        compiler_params=_params("arbitrary", "arbitrary"),
        name="swiglu_up",
    )(a, w1, w3)


def _resid_kernel(a_ref, w_ref, x_ref, g_ref, o_ref, *, nk):
    part = g_ref[...] * _dot(a_ref[...], w_ref[...])
    if nk == 1:
        o_ref[...] = x_ref[...] + part
        return

    @pl.when(pl.program_id(2) == 0)
    def _():
        o_ref[...] = x_ref[...]

    o_ref[...] += part


def _project_residual(a, w, x, gate, tm=1024, tn=512, tk=None, name="proj_resid"):
    m, k = a.shape
    n = w.shape[1]
    tk = k if tk is None else tk
    nk = k // tk
    return pl.pallas_call(
        functools.partial(_resid_kernel, nk=nk),
        grid=(m // tm, n // tn, nk),
        in_specs=[
            pl.BlockSpec((tm, tk), lambda i, j, kk: (i, kk)),
            pl.BlockSpec((tk, tn), lambda i, j, kk: (kk, j)),
            pl.BlockSpec((tm, tn), lambda i, j, kk: (i, j)),
            pl.BlockSpec((1, tn), lambda i, j, kk: (0, j)),
        ],
        out_specs=pl.BlockSpec((tm, tn), lambda i, j, kk: (i, j)),
        out_shape=jax.ShapeDtypeStruct((m, n), F32),
        compiler_params=_params("arbitrary", "arbitrary", "arbitrary"),
        name=name,
    )(a, w, x, gate)


CONV_HALO = 16


def _conv_proj_kernel(a_ref, ap_ref, wb_ref, wc_ref, wu_ref, cw_ref, o_ref, ax_ref):
    i = pl.program_id(0)
    h = CONV_HALO

    @pl.when(pl.program_id(1) == 0)
    def _():
        ax_ref[0:h, :] = ap_ref[...]
        ax_ref[h:, :] = a_ref[...]

    ax = ax_ref[...]
    b = _dot(a_ref[...], wb_ref[...])
    z = _dot(ax, wc_ref[...]) * _dot(ax, wu_ref[...])
    row = lax.broadcasted_iota(jnp.int32, z.shape, 0)
    z = jnp.where(jnp.logical_or(i > 0, row >= h), z, 0.0)
    z1 = pltpu.roll(z, 1, 0)
    z2 = pltpu.roll(z, 2, 0)
    cw = cw_ref[...]
    y = cw[2:3, :] * z[h:, :] + cw[1:2, :] * z1[h:, :] + cw[0:1, :] * z2[h:, :]
    o_ref[...] = (b * y).astype(o_ref.dtype)


def _conv_project(a, w, conv_w, tm=1024, tn=256):
    m, k = a.shape
    n = D_MODEL
    nb = n // tn
    hb = tm // CONV_HALO
    return pl.pallas_call(
        _conv_proj_kernel,
        grid=(m // tm, nb),
        in_specs=[
            pl.BlockSpec((tm, k), lambda i, j: (i, 0)),
            pl.BlockSpec((CONV_HALO, k), lambda i, j: (jnp.maximum(i * hb - 1, 0), 0)),
            pl.BlockSpec((k, tn), lambda i, j: (0, j)),
            pl.BlockSpec((k, tn), lambda i, j: (0, nb + j)),
            pl.BlockSpec((k, tn), lambda i, j: (0, 2 * nb + j)),
            pl.BlockSpec((3, tn), lambda i, j: (0, j)),
        ],
        out_specs=pl.BlockSpec((tm, tn), lambda i, j: (i, j)),
        out_shape=jax.ShapeDtypeStruct((m, n), BF16),
        scratch_shapes=[pltpu.VMEM((tm + CONV_HALO, k), BF16)],
        compiler_params=_params("arbitrary", "arbitrary"),
        name="conv_proj",
    )(a, a, w, w, w, conv_w)


RET_T = 512
RET_HP = 2


def _ret_kernel(lg_ref, q_ref, k_ref, v_ref, g_ref, o_ref, state_ref):
    @pl.when(pl.program_id(1) == 0)
    def _():
        state_ref[...] = jnp.zeros_like(state_ref)

    c = RET_CHUNK
    dh = RET_DIM
    inv_scale = dh ** -0.5
    ri = lax.broadcasted_iota(jnp.int32, (c, c), 0)
    ci = lax.broadcasted_iota(jnp.int32, (c, c), 1)
    rel = (ri - ci).astype(F32)
    pos = lax.broadcasted_iota(jnp.int32, (c, 1), 0).astype(F32)
    decays = []
    for hh in range(RET_HP):
        lg = lg_ref[pl.program_id(0) * RET_HP + hh]
        intra = jnp.where(rel >= 0, jnp.exp(lg * jnp.maximum(rel, 0.0)), 0.0) * inv_scale
        q_decay = jnp.exp(lg * (pos + 1.0))
        k_decay = jnp.exp(lg * (c - 1.0 - pos)) * inv_scale
        chunk_decay = jnp.exp(jnp.full((1, 1), lg * c, F32))
        decays.append((intra, q_decay, k_decay, chunk_decay))

    for t in range(RET_T // c):
        rows = slice(t * c, (t + 1) * c)
        for hh in range(RET_HP):
            intra, q_decay, k_decay, chunk_decay = decays[hh]
            cols = slice(hh * dh, (hh + 1) * dh)
            q = q_ref[rows, cols]
            k = k_ref[rows, cols]
            v = v_ref[rows, cols]
            state = state_ref[hh]
            scores = lax.dot_general(q, k, (((1,), (1,)), ((), ())), preferred_element_type=F32) * intra
            o = _dot(scores.astype(BF16), v)
            o = o + _dot(q, state.astype(BF16)) * q_decay
            kd = (k.astype(F32) * k_decay).astype(BF16)
            kv = lax.dot_general(kd, v, (((0,), (0,)), ((), ())), preferred_element_type=F32)
            state_ref[hh] = state * chunk_decay + kv
            mu = jnp.mean(o, axis=-1, keepdims=True)
            oc = o - mu
            var = jnp.mean(oc * oc, axis=-1, keepdims=True)
            on = oc * lax.rsqrt(var + EPS)
            o_ref[rows, cols] = (on * _silu(g_ref[rows, cols].astype(F32))).astype(o_ref.dtype)


def _retention(p_ret, log_g):
    m = p_ret.shape[0]
    w = RET_HP * RET_DIM
    groups = RET_HEADS // RET_HP

    def spec(part):
        return pl.BlockSpec((RET_T, w), lambda g, s: (s, part * groups + g))

    return pl.pallas_call(
        _ret_kernel,
        grid=(groups, m // RET_T),
        in_specs=[pl.BlockSpec(memory_space=pltpu.SMEM), spec(0), spec(1), spec(2), spec(3)],
        out_specs=pl.BlockSpec((RET_T, w), lambda g, s: (s, g)),
        out_shape=jax.ShapeDtypeStruct((m, RET_W + MOBA_W), BF16),
        scratch_shapes=[pltpu.VMEM((RET_HP, RET_DIM, RET_DIM), F32)],
        compiler_params=_params("arbitrary", "arbitrary"),
        name="retention",
    )(log_g, p_ret, p_ret, p_ret, p_ret)


MOBA_KC = 1024
MOBA_FILL = 2048
MOBA_HP = 2
MOBA_BLOCK_SHIFT = MOBA_BLOCK.bit_length() - 1


def _moba_kernel(sl_ref, q_ref, k_ref, v_ref, mix_ref, o_ref, ka_ref, va_ref, km_ref, s_ref, acc_ref):
    del mix_ref
    i = pl.program_id(1)
    bs = MOBA_BLOCK
    d = MOBA_DIM
    kc = MOBA_KC
    hp = MOBA_HP
    seq = k_ref.shape[0]
    heads = range(hp)

    @pl.when(i == 0)
    def _():
        km_ref[...] = jnp.zeros_like(km_ref)
        fr = MOBA_FILL
        lane = lax.broadcasted_iota(jnp.int32, (fr, d), 1)
        sub = lax.broadcasted_iota(jnp.int32, (d, kc), 0)
        key = lax.broadcasted_iota(jnp.int32, (d, kc), 1)

        def fill(c, carry):
            r = pl.multiple_of(c * fr, fr)
            ones_col = jnp.where(lane == 0, 1.0, 0.0).astype(BF16)
            for hh in heads:
                kb = k_ref[pl.ds(r, fr), hh * d:(hh + 1) * d]
                for t in range(fr // kc):
                    chunk = c * (fr // kc) + t
                    onehot = sub == jnp.right_shift(key + chunk * kc, MOBA_BLOCK_SHIFT)
                    ka_ref[hh, chunk, 0:d, :] = kb[t * kc:(t + 1) * kc, :].astype(F32).T.astype(BF16)
                    ka_ref[hh, chunk, d:2 * d, :] = jnp.where(onehot, 1.0, 0.0).astype(BF16)
                va_ref[hh, pl.ds(r, fr), 0:d] = v_ref[pl.ds(r, fr), hh * d:(hh + 1) * d]
                va_ref[hh, pl.ds(r, fr), d:2 * d] = ones_col
                kmean = jnp.mean(kb.astype(F32).reshape(fr // bs, bs, d), axis=1)
                km_ref[hh, pl.ds(pl.multiple_of(c * (fr // bs), fr // bs), fr // bs), :] = kmean
            return carry

        lax.fori_loop(0, seq // fr, fill, 0)

    scale = d ** -0.5
    slopes = [sl_ref[pl.program_id(0) * hp + hh] for hh in heads]

    q_augs = []
    for hh in heads:
        q = q_ref[:, hh * d:(hh + 1) * d]
        gate = lax.dot_general(q, km_ref[hh], (((1,), (1,)), ((), ())),
                               precision=lax.Precision.HIGHEST, preferred_element_type=F32)
        col = lax.broadcasted_iota(jnp.int32, gate.shape, 1)
        g = jnp.where(col < i, gate, -jnp.inf)
        bias = jnp.where(col == i, 0.0, NEG_BIG)
        for _ in range(MOBA_TOPK):
            mx = jnp.max(g, axis=1, keepdims=True)
            idx = jnp.min(jnp.where(g == mx, col, 2 * d), axis=1, keepdims=True)
            hit = col == idx
            bias = jnp.where(jnp.logical_and(hit, mx > -jnp.inf), 0.0, bias)
            g = jnp.where(hit, -jnp.inf, g)
        q_augs.append(jnp.concatenate([q.astype(BF16), bias.astype(BF16)], axis=1))

    def scores(j, causal, hh):
        r = pl.multiple_of(j * kc, kc)
        s = _dot(q_augs[hh], ka_ref[hh, j]) * scale
        rel = lax.broadcasted_iota(jnp.int32, (1, kc), 1) + (r - i * bs)
        s = s + slopes[hh] * rel.astype(F32)
        if causal:
            row = lax.broadcasted_iota(jnp.int32, (bs, kc), 0)
            s = jnp.where(rel <= row, s, NEG_BIG)
        s_ref[hh, j] = s
        part = s[:, 0:d]
        for t in range(1, kc // d):
            part = jnp.maximum(part, s[:, t * d:(t + 1) * d])
        return part

    def sweep(j, mxs, causal):
        return tuple(jnp.maximum(mxs[hh], scores(j, causal, hh)) for hh in heads)

    n_past = i // (kc // bs)
    mxs = tuple(jnp.full((bs, d), -jnp.inf, F32) for _ in heads)
    mxs = lax.fori_loop(0, n_past, lambda j, c: sweep(j, c, False), mxs)
    mxs = sweep(n_past, mxs, True)
    ms = [jnp.max(mxs[hh], axis=1, keepdims=True) for hh in heads]

    acc_ref[...] = jnp.zeros_like(acc_ref)

    def weighted(j, carry):
        r = pl.multiple_of(j * kc, kc)
        for hh in heads:
            p = jnp.exp(s_ref[hh, j] - ms[hh]).astype(BF16)
            acc_ref[hh] += _dot(p, va_ref[hh, pl.ds(r, kc), :])
        return carry

    lax.fori_loop(0, n_past + 1, weighted, 0)
    for hh in heads:
        acc = acc_ref[hh]
        o_ref[:, hh * d:(hh + 1) * d] = (acc[:, 0:d] / acc[:, d:d + 1]).astype(o_ref.dtype)


def _moba(q_all, kv_all, slopes, mix):
    m = q_all.shape[0]
    hp = MOBA_HP
    w = hp * MOBA_DIM
    groups = q_all.shape[1] // w
    staged_keys = pltpu.VMEM((hp, m // MOBA_KC, 2 * MOBA_DIM, MOBA_KC), BF16)
    staged_values = pltpu.VMEM((hp, m, 2 * MOBA_DIM), BF16)
    return pl.pallas_call(
        _moba_kernel,
        grid=(groups, m // MOBA_BLOCK),
        in_specs=[
            pl.BlockSpec(memory_space=pltpu.SMEM),
            pl.BlockSpec((MOBA_BLOCK, w), lambda g, i: (i, g)),
            pl.BlockSpec((m, w), lambda g, i: (0, g)),
            pl.BlockSpec((m, w), lambda g, i: (0, groups + g)),
            pl.BlockSpec(memory_space=pl.ANY),
        ],
        out_specs=pl.BlockSpec((MOBA_BLOCK, w), lambda g, i: (i, RET_W // w + g)),
        out_shape=jax.ShapeDtypeStruct(mix.shape, BF16),
        input_output_aliases={4: 0},
        scratch_shapes=[
            staged_keys,
            staged_values,
            pltpu.VMEM((hp, MOBA_DIM, MOBA_DIM), F32),
            pltpu.VMEM((hp, m // MOBA_KC, MOBA_BLOCK, MOBA_KC), F32),
            pltpu.VMEM((hp, MOBA_BLOCK, 2 * MOBA_DIM), F32),
        ],
        compiler_params=_params("arbitrary", "arbitrary"),
        name="moba",
    )(slopes, q_all, kv_all, kv_all, mix)


def _split_mod(mod):
    d = D_MODEL
    return mod[:, :d], mod[:, d:2 * d], mod[:, 2 * d:]


def _ffn(x, c_col, norm_g, mod_w, mod_b, w1, w3, w2):
    shift, scale, gate = _split_mod(_mod_vector(c_col, mod_w, mod_b))
    h = _norm_modulate(x, norm_g, scale, shift)
    u = _swiglu_up(h, w1, w3)
    return _project_residual(u, w2.astype(BF16), x, gate, tk=FFN_HIDDEN // 2, name="ffn_down")


def kernel(x, c, l0_mix_norm_g, l0_mix_mod_w, l0_mix_mod_b, l0_mix_w_in, l0_mix_w_out, l0_ffn_norm_g, l0_ffn_mod_w, l0_ffn_mod_b, l0_ffn_w1, l0_ffn_w3, l0_ffn_w2, l1_mix_norm_g, l1_mix_mod_w, l1_mix_mod_b, l1_mix_w_in, l1_mix_conv_w, l1_mix_w_out, l1_ffn_norm_g, l1_ffn_mod_w, l1_ffn_mod_b, l1_ffn_w1, l1_ffn_w3, l1_ffn_w2, final_norm_g):
    b, seq, d = x.shape
    xs = x.reshape(b * seq, d)
    c_col = c.reshape(d, 1)

    log_g = jnp.log1p(-jnp.exp2(-5.0 - jnp.arange(RET_HEADS, dtype=F32)))
    slopes = jnp.exp2(-8.0 * jnp.arange(1, MOBA_HEADS + 1, dtype=F32) / MOBA_HEADS)

    shift, scale, gate = _split_mod(_mod_vector(c_col, l0_mix_mod_w, l0_mix_mod_b))
    h = _norm_modulate(xs, l0_mix_norm_g, scale, shift)
    w_in = l0_mix_w_in
    p_ret = _project(h, w_in, 0, 4 * RET_W, BF16, name="mix0_in_ret")
    moba_q = _project(h, w_in, 4 * RET_W, MOBA_W, F32, name="mix0_in_q")
    moba_kv = _project(h, w_in, 4 * RET_W + MOBA_W, 2 * MOBA_W, BF16, name="mix0_in_kv")
    mix = _moba(moba_q, moba_kv, slopes, _retention(p_ret, log_g))
    xs = _project_residual(mix, l0_mix_w_out, xs, gate, name="mix0_out")
    xs = _ffn(xs, c_col, l0_ffn_norm_g, l0_ffn_mod_w, l0_ffn_mod_b, l0_ffn_w1, l0_ffn_w3, l0_ffn_w2)

    shift, scale, gate = _split_mod(_mod_vector(c_col, l1_mix_mod_w, l1_mix_mod_b))
    h = _norm_modulate(xs, l1_mix_norm_g, scale, shift)
    s = _conv_project(h, l1_mix_w_in.astype(BF16), l1_mix_conv_w.reshape(3, d))
    xs = _project_residual(s, l1_mix_w_out, xs, gate, name="mix1_out")
    xs = _ffn(xs, c_col, l1_ffn_norm_g, l1_ffn_mod_w, l1_ffn_mod_b, l1_ffn_w1, l1_ffn_w3, l1_ffn_w2)

    return _final_norm(xs, final_norm_g).reshape(b, seq, d)
```

```python
import functools

import jax
import jax.numpy as jnp
from jax import lax
from jax.experimental import pallas as pl
from jax.experimental.pallas import tpu as pltpu

D_MODEL = 4096
SEQ = 8192
RET_HEADS = 8
RET_DIM = 256
RET_CHUNK = 128
MOBA_HEADS = 16
MOBA_DIM = 128
MOBA_BLOCK = 256
MOBA_TOPK = 3
FFN_HIDDEN = 11008
EPS = 1e-6
RET_W = RET_HEADS * RET_DIM
MOBA_W = MOBA_HEADS * MOBA_DIM
N_BLOCKS = SEQ // MOBA_BLOCK

VMEM_LIMIT_BYTES = 56 * 1024 * 1024
NEG_BIG = -1e30

BF16 = jnp.bfloat16
F32 = jnp.float32


def _params(*sem):
    return pltpu.CompilerParams(dimension_semantics=sem, vmem_limit_bytes=VMEM_LIMIT_BYTES)


def _silu(v):
    return v * jax.nn.sigmoid(v)


MOD_TN = 512
MOD_KC = 128


def _mod_kernel(c_ref, w_ref, b_ref, o_ref, ca_ref):
    lanes = ca_ref.shape[1]

    @pl.when(pl.program_id(0) == 0)
    def _():
        ca_ref[...] = jnp.broadcast_to(_silu(c_ref[...]), ca_ref.shape)

    def body(k, acc):
        r = pl.multiple_of(k * MOD_KC, MOD_KC)
        ca = ca_ref[pl.ds(r, MOD_KC), :]
        p = w_ref[pl.ds(r, MOD_KC), :] * jnp.concatenate([ca] * (MOD_TN // lanes), axis=1)
        return acc + jnp.sum(p.reshape(MOD_KC // 8, 8, MOD_TN), axis=0)

    acc = lax.fori_loop(0, D_MODEL // MOD_KC, body, jnp.zeros((8, MOD_TN), F32), unroll=4)
    o_ref[...] = jnp.sum(acc, axis=0, keepdims=True) + b_ref[...]


def _mod_vector(c_col, w, b):
    n = w.shape[1]
    return pl.pallas_call(
        _mod_kernel,
        grid=(n // MOD_TN,),
        in_specs=[
            pl.BlockSpec((D_MODEL, 1), lambda j: (0, 0)),
            pl.BlockSpec((D_MODEL, MOD_TN), lambda j: (0, j)),
            pl.BlockSpec((1, MOD_TN), lambda j: (0, j)),
        ],
        out_specs=pl.BlockSpec((1, MOD_TN), lambda j: (0, j)),
        out_shape=jax.ShapeDtypeStruct((1, n), F32),
        scratch_shapes=[pltpu.VMEM((D_MODEL, 128), F32)],
        compiler_params=_params("arbitrary"),
        name="adaln_mod",
    )(c_col, w, b.reshape(1, n))


NORM_TM = 256


NORM_RC = 16


def _norm_rows(x_ref, o_ref, mult, add):
    def body(r, carry):
        rows = pl.ds(pl.multiple_of(r * NORM_RC, NORM_RC), NORM_RC)
        x = x_ref[rows, :]
        ms = jnp.mean(x * x, axis=-1, keepdims=True)
        y = x * lax.rsqrt(ms + EPS) * mult
        if add is not None:
            y = y + add
        o_ref[rows, :] = y.astype(o_ref.dtype)
        return carry

    lax.fori_loop(0, x_ref.shape[0] // NORM_RC, body, 0, unroll=8)


def _normmod_kernel(x_ref, g_ref, sc_ref, sh_ref, o_ref):
    _norm_rows(x_ref, o_ref, g_ref[...] * (1.0 + sc_ref[...]), sh_ref[...])


def _norm_kernel(x_ref, g_ref, o_ref):
    _norm_rows(x_ref, o_ref, g_ref[...], None)


def _norm_modulate(x, g, scale, shift):
    m, d = x.shape
    row = pl.BlockSpec((NORM_TM, d), lambda i: (i, 0))
    vec = pl.BlockSpec((1, d), lambda i: (0, 0))
    return pl.pallas_call(
        _normmod_kernel,
        grid=(m // NORM_TM,),
        in_specs=[row, vec, vec, vec],
        out_specs=row,
        out_shape=jax.ShapeDtypeStruct((m, d), BF16),
        compiler_params=_params("arbitrary"),
        name="norm_modulate",
    )(x, g.reshape(1, d), scale, shift)


def _final_norm(x, g):
    m, d = x.shape
    row = pl.BlockSpec((NORM_TM, d), lambda i: (i, 0))
    vec = pl.BlockSpec((1, d), lambda i: (0, 0))
    return pl.pallas_call(
        _norm_kernel,
        grid=(m // NORM_TM,),
        in_specs=[row, vec],
        out_specs=row,
        out_shape=jax.ShapeDtypeStruct((m, d), F32),
        compiler_params=_params("arbitrary"),
        name="final_norm",
    )(x, g.reshape(1, d))


def _dot(a, b):
    return jnp.dot(a, b.astype(BF16), preferred_element_type=F32)


def _proj_kernel(a_ref, w_ref, o_ref):
    o_ref[...] = _dot(a_ref[...], w_ref[...]).astype(o_ref.dtype)


def _project(a, w, col0, ncols, out_dtype, tm=1024, tn=512, name="proj"):
    m, k = a.shape
    cb = col0 // tn
    return pl.pallas_call(
        _proj_kernel,
        grid=(m // tm, ncols // tn),
        in_specs=[
            pl.BlockSpec((tm, k), lambda i, j: (i, 0)),
            pl.BlockSpec((k, tn), lambda i, j: (0, cb + j)),
        ],
        out_specs=pl.BlockSpec((tm, tn), lambda i, j: (i, j)),
        out_shape=jax.ShapeDtypeStruct((m, ncols), out_dtype),
        compiler_params=_params("arbitrary", "arbitrary"),
        name=name,
    )(a, w)


def _swiglu_kernel(a_ref, w1_ref, w3_ref, o_ref):
    a = a_ref[...]
    h1 = _dot(a, w1_ref[...])
    h3 = _dot(a, w3_ref[...])
    o_ref[...] = (_silu(h1) * h3).astype(o_ref.dtype)


def _swiglu_up(a, w1, w3, tm=2048, tn=256):
    m, k = a.shape
    n = w1.shape[1]
    wspec = pl.BlockSpec((k, tn), lambda i, j: (0, j))
    return pl.pallas_call(
        _swiglu_kernel,
        grid=(m // tm, n // tn),
        in_specs=[pl.BlockSpec((tm, k), lambda i, j: (i, 0)), wspec, wspec],
        out_specs=pl.BlockSpec((tm, tn), lambda i, j: (i, j)),
        out_shape=jax.ShapeDtypeStruct((m, n), BF16),
        compiler_params=_params("arbitrary", "arbitrary"),
        name="swiglu_up",
    )(a, w1, w3)


def _resid_kernel(a_ref, w_ref, x_ref, g_ref, o_ref, *, nk):
    part = g_ref[...] * _dot(a_ref[...], w_ref[...])
    if nk == 1:
        o_ref[...] = x_ref[...] + part
        return

    @pl.when(pl.program_id(2) == 0)
    def _():
        o_ref[...] = x_ref[...]

    o_ref[...] += part


def _project_residual(a, w, x, gate, tm=1024, tn=512, tk=None, name="proj_resid"):
    m, k = a.shape
    n = w.shape[1]
    tk = k if tk is None else tk
    nk = k // tk
    return pl.pallas_call(
        functools.partial(_resid_kernel, nk=nk),
        grid=(m // tm, n // tn, nk),
        in_specs=[
            pl.BlockSpec((tm, tk), lambda i, j, kk: (i, kk)),
            pl.BlockSpec((tk, tn), lambda i, j, kk: (kk, j)),
            pl.BlockSpec((tm, tn), lambda i, j, kk: (i, j)),
            pl.BlockSpec((1, tn), lambda i, j, kk: (0, j)),
        ],
        out_specs=pl.BlockSpec((tm, tn), lambda i, j, kk: (i, j)),
        out_shape=jax.ShapeDtypeStruct((m, n), F32),
        compiler_params=_params("arbitrary", "arbitrary", "arbitrary"),
        name=name,
    )(a, w, x, gate)


def _resid_cat_kernel(a1_ref, a2_ref, w_ref, x_ref, g_ref, o_ref):
    k1 = a1_ref.shape[1]
    part = _dot(a1_ref[...], w_ref[0:k1, :]) + _dot(a2_ref[...], w_ref[k1:, :])
    o_ref[...] = x_ref[...] + g_ref[...] * part


def _project_residual_cat(a1, a2, w, x, gate, tm=1024, tn=512, name="proj_resid_cat"):
    m, k1 = a1.shape
    k2 = a2.shape[1]
    n = w.shape[1]
    return pl.pallas_call(
        _resid_cat_kernel,
        grid=(m // tm, n // tn),
        in_specs=[
            pl.BlockSpec((tm, k1), lambda i, j: (i, 0)),
            pl.BlockSpec((tm, k2), lambda i, j: (i, 0)),
            pl.BlockSpec((k1 + k2, tn), lambda i, j: (0, j)),
            pl.BlockSpec((tm, tn), lambda i, j: (i, j)),
            pl.BlockSpec((1, tn), lambda i, j: (0, j)),
        ],
        out_specs=pl.BlockSpec((tm, tn), lambda i, j: (i, j)),
        out_shape=jax.ShapeDtypeStruct((m, n), F32),
        compiler_params=_params("arbitrary", "arbitrary"),
        name=name,
    )(a1, a2, w, x, gate)


CONV_HALO = 16


def _conv_proj_kernel(a_ref, ap_ref, wb_ref, wc_ref, wu_ref, cw_ref, o_ref, ax_ref):
    i = pl.program_id(0)
    h = CONV_HALO

    @pl.when(pl.program_id(1) == 0)
    def _():
        ax_ref[0:h, :] = ap_ref[...]
        ax_ref[h:, :] = a_ref[...]

    ax = ax_ref[...]
    b = _dot(a_ref[...], wb_ref[...])
    z = _dot(ax, wc_ref[...]) * _dot(ax, wu_ref[...])
    row = lax.broadcasted_iota(jnp.int32, z.shape, 0)
    z = jnp.where(jnp.logical_or(i > 0, row >= h), z, 0.0)
    z1 = pltpu.roll(z, 1, 0)
    z2 = pltpu.roll(z, 2, 0)
    cw = cw_ref[...]
    y = cw[2:3, :] * z[h:, :] + cw[1:2, :] * z1[h:, :] + cw[0:1, :] * z2[h:, :]
    o_ref[...] = (b * y).astype(o_ref.dtype)


def _conv_project(a, w, conv_w, tm=1024, tn=256):
    m, k = a.shape
    n = D_MODEL
    nb = n // tn
    hb = tm // CONV_HALO
    return pl.pallas_call(
        _conv_proj_kernel,
        grid=(m // tm, nb),
        in_specs=[
            pl.BlockSpec((tm, k), lambda i, j: (i, 0)),
            pl.BlockSpec((CONV_HALO, k), lambda i, j: (jnp.maximum(i * hb - 1, 0), 0)),
            pl.BlockSpec((k, tn), lambda i, j: (0, j)),
            pl.BlockSpec((k, tn), lambda i, j: (0, nb + j)),
            pl.BlockSpec((k, tn), lambda i, j: (0, 2 * nb + j)),
            pl.BlockSpec((3, tn), lambda i, j: (0, j)),
        ],
        out_specs=pl.BlockSpec((tm, tn), lambda i, j: (i, j)),
        out_shape=jax.ShapeDtypeStruct((m, n), BF16),
        scratch_shapes=[pltpu.VMEM((tm + CONV_HALO, k), BF16)],
        compiler_params=_params("arbitrary", "arbitrary"),
        name="conv_proj",
    )(a, a, w, w, w, conv_w)


RET_T = 512
RET_HP = 2


def _ret_kernel(lg_ref, q_ref, k_ref, v_ref, g_ref, o_ref, state_ref):
    @pl.when(pl.program_id(1) == 0)
    def _():
        state_ref[...] = jnp.zeros_like(state_ref)

    c = RET_CHUNK
    dh = RET_DIM
    inv_scale = dh ** -0.5
    ri = lax.broadcasted_iota(jnp.int32, (c, c), 0)
    ci = lax.broadcasted_iota(jnp.int32, (c, c), 1)
    rel = (ri - ci).astype(F32)
    pos = lax.broadcasted_iota(jnp.int32, (c, 1), 0).astype(F32)
    decays = []
    for hh in range(RET_HP):
        lg = lg_ref[pl.program_id(0) * RET_HP + hh]
        intra = jnp.where(rel >= 0, jnp.exp(lg * jnp.maximum(rel, 0.0)), 0.0) * inv_scale
        q_decay = jnp.exp(lg * (pos + 1.0))
        k_decay = jnp.exp(lg * (c - 1.0 - pos)) * inv_scale
        chunk_decay = jnp.exp(jnp.full((1, 1), lg * c, F32))
        decays.append((intra, q_decay, k_decay, chunk_decay))

    for t in range(RET_T // c):
        rows = slice(t * c, (t + 1) * c)
        for hh in range(RET_HP):
            intra, q_decay, k_decay, chunk_decay = decays[hh]
            cols = slice(hh * dh, (hh + 1) * dh)
            q = q_ref[rows, cols]
            k = k_ref[rows, cols]
            v = v_ref[rows, cols]
            state = state_ref[hh]
            scores = lax.dot_general(q, k, (((1,), (1,)), ((), ())), preferred_element_type=F32) * intra
            o = _dot(scores.astype(BF16), v)
            o = o + _dot(q, state.astype(BF16)) * q_decay
            kd = (k.astype(F32) * k_decay).astype(BF16)
            kv = lax.dot_general(kd, v, (((0,), (0,)), ((), ())), preferred_element_type=F32)
            state_ref[hh] = state * chunk_decay + kv
            mu = jnp.mean(o, axis=-1, keepdims=True)
            oc = o - mu
            var = jnp.mean(oc * oc, axis=-1, keepdims=True)
            on = oc * lax.rsqrt(var + EPS)
            o_ref[rows, cols] = (on * _silu(g_ref[rows, cols].astype(F32))).astype(o_ref.dtype)


def _retention(p_ret, log_g):
    m = p_ret.shape[0]
    w = RET_HP * RET_DIM
    groups = RET_HEADS // RET_HP

    def spec(part):
        return pl.BlockSpec((RET_T, w), lambda g, s: (s, part * groups + g))

    return pl.pallas_call(
        _ret_kernel,
        grid=(groups, m // RET_T),
        in_specs=[pl.BlockSpec(memory_space=pltpu.SMEM), spec(0), spec(1), spec(2), spec(3)],
        out_specs=pl.BlockSpec((RET_T, w), lambda g, s: (s, g)),
        out_shape=jax.ShapeDtypeStruct((m, RET_W), BF16),
        scratch_shapes=[pltpu.VMEM((RET_HP, RET_DIM, RET_DIM), F32)],
        compiler_params=_params("arbitrary", "arbitrary"),
        name="retention",
    )(log_g, p_ret, p_ret, p_ret, p_ret)


MOBA_KC = 1024
MOBA_FILL = 2048
MOBA_HP = 2
MOBA_BLOCK_SHIFT = MOBA_BLOCK.bit_length() - 1


def _moba_kernel(sl_ref, q_ref, k_ref, v_ref, o_ref, ka_ref, va_ref, km_ref, s_ref, acc_ref):
    i = pl.program_id(1)
    bs = MOBA_BLOCK
    d = MOBA_DIM
    kc = MOBA_KC
    hp = MOBA_HP
    seq = k_ref.shape[0]
    heads = range(hp)

    @pl.when(i == 0)
    def _():
        km_ref[...] = jnp.zeros_like(km_ref)
        fr = MOBA_FILL
        lane = lax.broadcasted_iota(jnp.int32, (fr, d), 1)
        sub = lax.broadcasted_iota(jnp.int32, (d, kc), 0)
        key = lax.broadcasted_iota(jnp.int32, (d, kc), 1)

        def fill(c, carry):
            r = pl.multiple_of(c * fr, fr)
            ones_col = jnp.where(lane == 0, 1.0, 0.0).astype(BF16)
            for hh in heads:
                kb = k_ref[pl.ds(r, fr), hh * d:(hh + 1) * d]
                for t in range(fr // kc):
                    chunk = c * (fr // kc) + t
                    onehot = sub == jnp.right_shift(key + chunk * kc, MOBA_BLOCK_SHIFT)
                    ka_ref[hh, chunk, 0:d, :] = kb[t * kc:(t + 1) * kc, :].astype(F32).T.astype(BF16)
                    ka_ref[hh, chunk, d:2 * d, :] = jnp.where(onehot, 1.0, 0.0).astype(BF16)
                va_ref[hh, pl.ds(r, fr), 0:d] = v_ref[pl.ds(r, fr), hh * d:(hh + 1) * d]
                va_ref[hh, pl.ds(r, fr), d:2 * d] = ones_col
                kmean = jnp.mean(kb.astype(F32).reshape(fr // bs, bs, d), axis=1)
                km_ref[hh, pl.ds(pl.multiple_of(c * (fr // bs), fr // bs), fr // bs), :] = kmean
            return carry

        lax.fori_loop(0, seq // fr, fill, 0)

    scale = d ** -0.5
    slopes = [sl_ref[pl.program_id(0) * hp + hh] for hh in heads]

    q_augs = []
    for hh in heads:
        q = q_ref[:, hh * d:(hh + 1) * d]
        gate = lax.dot_general(q, km_ref[hh], (((1,), (1,)), ((), ())),
                               precision=lax.Precision.HIGHEST, preferred_element_type=F32)
        col = lax.broadcasted_iota(jnp.int32, gate.shape, 1)
        colf = col.astype(F32)
        g = jnp.where(col < i, gate, -jnp.inf)
        bias = jnp.where(col == i, 0.0, NEG_BIG)
        for _ in range(MOBA_TOPK):
            mx = jnp.max(g, axis=1, keepdims=True)
            idx = jnp.min(jnp.where(g == mx, colf, 2.0 * d), axis=1, keepdims=True)
            hit = colf == idx
            bias = jnp.where(jnp.logical_and(hit, mx > -jnp.inf), 0.0, bias)
            g = jnp.where(hit, -jnp.inf, g)
        q_augs.append(jnp.concatenate([q.astype(BF16), bias.astype(BF16)], axis=1))

    def scores(j, causal, hh):
        r = pl.multiple_of(j * kc, kc)
        s = _dot(q_augs[hh], ka_ref[hh, j]) * scale
        rel = lax.broadcasted_iota(jnp.int32, (1, kc), 1) + (r - i * bs)
        s = s + slopes[hh] * rel.astype(F32)
        if causal:
            row = lax.broadcasted_iota(jnp.int32, (bs, kc), 0)
            s = jnp.where(rel <= row, s, NEG_BIG)
        s_ref[hh, j] = s
        part = s[:, 0:d]
        for t in range(1, kc // d):
            part = jnp.maximum(part, s[:, t * d:(t + 1) * d])
        return part

    def sweep(j, mxs, causal):
        return tuple(jnp.maximum(mxs[hh], scores(j, causal, hh)) for hh in heads)

    n_past = i // (kc // bs)
    mxs = tuple(jnp.full((bs, d), -jnp.inf, F32) for _ in heads)
    mxs = lax.fori_loop(0, n_past, lambda j, c: sweep(j, c, False), mxs)
    mxs = sweep(n_past, mxs, True)
    ms = [jnp.max(mxs[hh], axis=1, keepdims=True) for hh in heads]

    acc_ref[...] = jnp.zeros_like(acc_ref)

    def weighted(j, carry):
        r = pl.multiple_of(j * kc, kc)
        for hh in heads:
            p = jnp.exp(s_ref[hh, j] - ms[hh]).astype(BF16)
            acc_ref[hh] += _dot(p, va_ref[hh, pl.ds(r, kc), :])
        return carry

    lax.fori_loop(0, n_past + 1, weighted, 0)
    for hh in heads:
        acc = acc_ref[hh]
        o_ref[:, hh * d:(hh + 1) * d] = (acc[:, 0:d] / acc[:, d:d + 1]).astype(o_ref.dtype)


def _moba(q_all, kv_all, slopes):
    m = q_all.shape[0]
    hp = MOBA_HP
    w = hp * MOBA_DIM
    groups = q_all.shape[1] // w
    staged_keys = pltpu.VMEM((hp, m // MOBA_KC, 2 * MOBA_DIM, MOBA_KC), BF16)
    staged_values = pltpu.VMEM((hp, m, 2 * MOBA_DIM), BF16)
    return pl.pallas_call(
        _moba_kernel,
        grid=(groups, m // MOBA_BLOCK),
        in_specs=[
            pl.BlockSpec(memory_space=pltpu.SMEM),
            pl.BlockSpec((MOBA_BLOCK, w), lambda g, i: (i, g)),
            pl.BlockSpec((m, w), lambda g, i: (0, g)),
            pl.BlockSpec((m, w), lambda g, i: (0, groups + g)),
        ],
        out_specs=pl.BlockSpec((MOBA_BLOCK, w), lambda g, i: (i, g)),
        out_shape=jax.ShapeDtypeStruct(q_all.shape, BF16),
        scratch_shapes=[
            staged_keys,
            staged_values,
            pltpu.VMEM((hp, MOBA_DIM, MOBA_DIM), F32),
            pltpu.VMEM((hp, m // MOBA_KC, MOBA_BLOCK, MOBA_KC), F32),
            pltpu.VMEM((hp, MOBA_BLOCK, 2 * MOBA_DIM), F32),
        ],
        compiler_params=_params("arbitrary", "arbitrary"),
        name="moba",
    )(slopes, q_all, kv_all, kv_all)


def _split_mod(mod):
    d = D_MODEL
    return mod[:, :d], mod[:, d:2 * d], mod[:, 2 * d:]


def _ffn(x, c_col, norm_g, mod_w, mod_b, w1, w3, w2):
    shift, scale, gate = _split_mod(_mod_vector(c_col, mod_w, mod_b))
    h = _norm_modulate(x, norm_g, scale, shift)
    u = _swiglu_up(h, w1, w3)
    return _project_residual(u, w2.astype(BF16), x, gate, tm=512, name="ffn_down")


def kernel(x, c, l0_mix_norm_g, l0_mix_mod_w, l0_mix_mod_b, l0_mix_w_in, l0_mix_w_out, l0_ffn_norm_g, l0_ffn_mod_w, l0_ffn_mod_b, l0_ffn_w1, l0_ffn_w3, l0_ffn_w2, l1_mix_norm_g, l1_mix_mod_w, l1_mix_mod_b, l1_mix_w_in, l1_mix_conv_w, l1_mix_w_out, l1_ffn_norm_g, l1_ffn_mod_w, l1_ffn_mod_b, l1_ffn_w1, l1_ffn_w3, l1_ffn_w2, final_norm_g):
    b, seq, d = x.shape
    xs = x.reshape(b * seq, d)
    c_col = c.reshape(d, 1)

    log_g = jnp.log1p(-jnp.exp2(-5.0 - jnp.arange(RET_HEADS, dtype=F32)))
    slopes = jnp.exp2(-8.0 * jnp.arange(1, MOBA_HEADS + 1, dtype=F32) / MOBA_HEADS)

    shift, scale, gate = _split_mod(_mod_vector(c_col, l0_mix_mod_w, l0_mix_mod_b))
    h = _norm_modulate(xs, l0_mix_norm_g, scale, shift)
    w_in = l0_mix_w_in
    p_ret = _project(h, w_in, 0, 4 * RET_W, BF16, name="mix0_in_ret")
    moba_q = _project(h, w_in, 4 * RET_W, MOBA_W, F32, name="mix0_in_q")
    moba_kv = _project(h, w_in, 4 * RET_W + MOBA_W, 2 * MOBA_W, BF16, name="mix0_in_kv")
    ret = _retention(p_ret, log_g)
    att = _moba(moba_q, moba_kv, slopes)
    xs = _project_residual_cat(ret, att, l0_mix_w_out, xs, gate, name="mix0_out")
    xs = _ffn(xs, c_col, l0_ffn_norm_g, l0_ffn_mod_w, l0_ffn_mod_b, l0_ffn_w1, l0_ffn_w3, l0_ffn_w2)

    shift, scale, gate = _split_mod(_mod_vector(c_col, l1_mix_mod_w, l1_mix_mod_b))
    h = _norm_modulate(xs, l1_mix_norm_g, scale, shift)
    s = _conv_project(h, l1_mix_w_in.astype(BF16), l1_mix_conv_w.reshape(3, d))
    xs = _project_residual(s, l1_mix_w_out, xs, gate, name="mix1_out")
    xs = _ffn(xs, c_col, l1_ffn_norm_g, l1_ffn_mod_w, l1_ffn_mod_b, l1_ffn_w1, l1_ffn_w3, l1_ffn_w2)

    return _final_norm(xs, final_norm_g).reshape(b, seq, d)
```

```python
import functools

import jax
import jax.numpy as jnp
from jax import lax
from jax.experimental import pallas as pl
from jax.experimental.pallas import tpu as pltpu

D_MODEL = 4096
SEQ = 8192
RET_HEADS = 8
RET_DIM = 256
RET_CHUNK = 128
MOBA_HEADS = 16
MOBA_DIM = 128
MOBA_BLOCK = 256
MOBA_TOPK = 3
FFN_HIDDEN = 11008
EPS = 1e-6
RET_W = RET_HEADS * RET_DIM
MOBA_W = MOBA_HEADS * MOBA_DIM
N_BLOCKS = SEQ // MOBA_BLOCK

VMEM_LIMIT_BYTES = 56 * 1024 * 1024
NEG_BIG = -1e30

BF16 = jnp.bfloat16
F32 = jnp.float32


def _params(*sem):
    return pltpu.CompilerParams(dimension_semantics=sem, vmem_limit_bytes=VMEM_LIMIT_BYTES)


def _silu(v):
    return v * jax.nn.sigmoid(v)


MOD_TN = 512
MOD_KC = 128


def _mod_kernel(c_ref, w_ref, b_ref, o_ref, ca_ref):
    lanes = ca_ref.shape[1]

    @pl.when(pl.program_id(0) == 0)
    def _():
        ca_ref[...] = jnp.broadcast_to(_silu(c_ref[...]), ca_ref.shape)

    def body(k, acc):
        r = pl.multiple_of(k * MOD_KC, MOD_KC)
        ca = ca_ref[pl.ds(r, MOD_KC), :]
        p = w_ref[pl.ds(r, MOD_KC), :] * jnp.concatenate([ca] * (MOD_TN // lanes), axis=1)
        return acc + jnp.sum(p.reshape(MOD_KC // 8, 8, MOD_TN), axis=0)

    acc = lax.fori_loop(0, D_MODEL // MOD_KC, body, jnp.zeros((8, MOD_TN), F32), unroll=4)
    o_ref[...] = jnp.sum(acc, axis=0, keepdims=True) + b_ref[...]


def _mod_vector(c_col, w, b):
    n = w.shape[1]
    return pl.pallas_call(
        _mod_kernel,
        grid=(n // MOD_TN,),
        in_specs=[
            pl.BlockSpec((D_MODEL, 1), lambda j: (0, 0)),
            pl.BlockSpec((D_MODEL, MOD_TN), lambda j: (0, j)),
            pl.BlockSpec((1, MOD_TN), lambda j: (0, j)),
        ],
        out_specs=pl.BlockSpec((1, MOD_TN), lambda j: (0, j)),
        out_shape=jax.ShapeDtypeStruct((1, n), F32),
        scratch_shapes=[pltpu.VMEM((D_MODEL, 128), F32)],
        compiler_params=_params("arbitrary"),
        name="adaln_mod",
    )(c_col, w, b.reshape(1, n))


NORM_TM = 256


NORM_RC = 16


def _norm_rows(x_ref, o_ref, mult, add):
    def body(r, carry):
        rows = pl.ds(pl.multiple_of(r * NORM_RC, NORM_RC), NORM_RC)
        x = x_ref[rows, :]
        ms = jnp.mean(x * x, axis=-1, keepdims=True)
        y = x * lax.rsqrt(ms + EPS) * mult
        if add is not None:
            y = y + add
        o_ref[rows, :] = y.astype(o_ref.dtype)
        return carry

    lax.fori_loop(0, x_ref.shape[0] // NORM_RC, body, 0, unroll=8)


def _normmod_kernel(x_ref, g_ref, sc_ref, sh_ref, o_ref):
    _norm_rows(x_ref, o_ref, g_ref[...] * (1.0 + sc_ref[...]), sh_ref[...])


def _norm_kernel(x_ref, g_ref, o_ref):
    _norm_rows(x_ref, o_ref, g_ref[...], None)


def _norm_modulate(x, g, scale, shift):
    m, d = x.shape
    row = pl.BlockSpec((NORM_TM, d), lambda i: (i, 0))
    vec = pl.BlockSpec((1, d), lambda i: (0, 0))
    return pl.pallas_call(
        _normmod_kernel,
        grid=(m // NORM_TM,),
        in_specs=[row, vec, vec, vec],
        out_specs=row,
        out_shape=jax.ShapeDtypeStruct((m, d), BF16),
        compiler_params=_params("arbitrary"),
        name="norm_modulate",
    )(x, g.reshape(1, d), scale, shift)


def _final_norm(x, g):
    m, d = x.shape
    row = pl.BlockSpec((NORM_TM, d), lambda i: (i, 0))
    vec = pl.BlockSpec((1, d), lambda i: (0, 0))
    return pl.pallas_call(
        _norm_kernel,
        grid=(m // NORM_TM,),
        in_specs=[row, vec],
        out_specs=row,
        out_shape=jax.ShapeDtypeStruct((m, d), F32),
        compiler_params=_params("arbitrary"),
        name="final_norm",
    )(x, g.reshape(1, d))


def _dot(a, b):
    return jnp.dot(a, b.astype(BF16), preferred_element_type=F32)


def _proj_kernel(a_ref, w_ref, o_ref):
    o_ref[...] = _dot(a_ref[...], w_ref[...]).astype(o_ref.dtype)


def _project(a, w, col0, ncols, out_dtype, tm=1024, tn=512, name="proj"):
    m, k = a.shape
    cb = col0 // tn
    return pl.pallas_call(
        _proj_kernel,
        grid=(m // tm, ncols // tn),
        in_specs=[
            pl.BlockSpec((tm, k), lambda i, j: (i, 0)),
            pl.BlockSpec((k, tn), lambda i, j: (0, cb + j)),
        ],
        out_specs=pl.BlockSpec((tm, tn), lambda i, j: (i, j)),
        out_shape=jax.ShapeDtypeStruct((m, ncols), out_dtype),
        compiler_params=_params("arbitrary", "arbitrary"),
        name=name,
    )(a, w)


def _swiglu_kernel(a_ref, w1_ref, w3_ref, o_ref):
    a = a_ref[...]
    h1 = _dot(a, w1_ref[...])
    h3 = _dot(a, w3_ref[...])
    o_ref[...] = (_silu(h1) * h3).astype(o_ref.dtype)


def _swiglu_up(a, w1, w3, tm=2048, tn=256):
    m, k = a.shape
    n = w1.shape[1]
    wspec = pl.BlockSpec((k, tn), lambda i, j: (0, j))
    return pl.pallas_call(
        _swiglu_kernel,
        grid=(m // tm, n // tn),
        in_specs=[pl.BlockSpec((tm, k), lambda i, j: (i, 0)), wspec, wspec],
        out_specs=pl.BlockSpec((tm, tn), lambda i, j: (i, j)),
        out_shape=jax.ShapeDtypeStruct((m, n), BF16),
        compiler_params=_params("arbitrary", "arbitrary"),
        name="swiglu_up",
    )(a, w1, w3)


def _resid_kernel(a_ref, w_ref, x_ref, g_ref, o_ref, *, nk):
    part = g_ref[...] * _dot(a_ref[...], w_ref[...])
    if nk == 1:
        o_ref[...] = x_ref[...] + part
        return

    @pl.when(pl.program_id(2) == 0)
    def _():
        o_ref[...] = x_ref[...]

    o_ref[...] += part


def _project_residual(a, w, x, gate, tm=1024, tn=512, tk=None, name="proj_resid"):
    m, k = a.shape
    n = w.shape[1]
    tk = k if tk is None else tk
    nk = k // tk
    return pl.pallas_call(
        functools.partial(_resid_kernel, nk=nk),
        grid=(m // tm, n // tn, nk),
        in_specs=[
            pl.BlockSpec((tm, tk), lambda i, j, kk: (i, kk)),
            pl.BlockSpec((tk, tn), lambda i, j, kk: (kk, j)),
            pl.BlockSpec((tm, tn), lambda i, j, kk: (i, j)),
            pl.BlockSpec((1, tn), lambda i, j, kk: (0, j)),
        ],
        out_specs=pl.BlockSpec((tm, tn), lambda i, j, kk: (i, j)),
        out_shape=jax.ShapeDtypeStruct((m, n), F32),
        compiler_params=_params("arbitrary", "arbitrary", "arbitrary"),
        name=name,
    )(a, w, x, gate)


def _resid_cat_kernel(a1_ref, a2_ref, w_ref, x_ref, g_ref, o_ref):
    k1 = a1_ref.shape[1]
    part = _dot(a1_ref[...], w_ref[0:k1, :]) + _dot(a2_ref[...], w_ref[k1:, :])
    o_ref[...] = x_ref[...] + g_ref[...] * part


def _project_residual_cat(a1, a2, w, x, gate, tm=1024, tn=512, name="proj_resid_cat"):
    m, k1 = a1.shape
    k2 = a2.shape[1]
    n = w.shape[1]
    return pl.pallas_call(
        _resid_cat_kernel,
        grid=(m // tm, n // tn),
        in_specs=[
            pl.BlockSpec((tm, k1), lambda i, j: (i, 0)),
            pl.BlockSpec((tm, k2), lambda i, j: (i, 0)),
            pl.BlockSpec((k1 + k2, tn), lambda i, j: (0, j)),
            pl.BlockSpec((tm, tn), lambda i, j: (i, j)),
            pl.BlockSpec((1, tn), lambda i, j: (0, j)),
        ],
        out_specs=pl.BlockSpec((tm, tn), lambda i, j: (i, j)),
        out_shape=jax.ShapeDtypeStruct((m, n), F32),
        compiler_params=_params("arbitrary", "arbitrary"),
        name=name,
    )(a1, a2, w, x, gate)


CONV_HALO = 16


def _conv_proj_kernel(a_ref, ap_ref, wb_ref, wc_ref, wu_ref, cw_ref, o_ref, ax_ref):
    i = pl.program_id(0)
    h = CONV_HALO

    @pl.when(pl.program_id(1) == 0)
    def _():
        ax_ref[0:h, :] = ap_ref[...]
        ax_ref[h:, :] = a_ref[...]

    ax = ax_ref[...]
    b = _dot(a_ref[...], wb_ref[...])
    z = _dot(ax, wc_ref[...]) * _dot(ax, wu_ref[...])
    row = lax.broadcasted_iota(jnp.int32, z.shape, 0)
    z = jnp.where(jnp.logical_or(i > 0, row >= h), z, 0.0)
    z1 = pltpu.roll(z, 1, 0)
    z2 = pltpu.roll(z, 2, 0)
    cw = cw_ref[...]
    y = cw[2:3, :] * z[h:, :] + cw[1:2, :] * z1[h:, :] + cw[0:1, :] * z2[h:, :]
    o_ref[...] = (b * y).astype(o_ref.dtype)


def _conv_project(a, w, conv_w, tm=1024, tn=256):
    m, k = a.shape
    n = D_MODEL
    nb = n // tn
    hb = tm // CONV_HALO
    return pl.pallas_call(
        _conv_proj_kernel,
        grid=(m // tm, nb),
        in_specs=[
            pl.BlockSpec((tm, k), lambda i, j: (i, 0), pipeline_mode=pl.Buffered(1)),
            pl.BlockSpec((CONV_HALO, k), lambda i, j: (jnp.maximum(i * hb - 1, 0), 0)),
            pl.BlockSpec((k, tn), lambda i, j: (0, j)),
            pl.BlockSpec((k, tn), lambda i, j: (0, nb + j)),
            pl.BlockSpec((k, tn), lambda i, j: (0, 2 * nb + j)),
            pl.BlockSpec((3, tn), lambda i, j: (0, j)),
        ],
        out_specs=pl.BlockSpec((tm, tn), lambda i, j: (i, j)),
        out_shape=jax.ShapeDtypeStruct((m, n), BF16),
        scratch_shapes=[pltpu.VMEM((tm + CONV_HALO, k), BF16)],
        compiler_params=_params("arbitrary", "arbitrary"),
        name="conv_proj",
    )(a, a, w, w, w, conv_w)


RET_T = 512
RET_HP = 2


def _ret_kernel(lg_ref, q_ref, k_ref, v_ref, g_ref, o_ref, state_ref):
    @pl.when(pl.program_id(1) == 0)
    def _():
        state_ref[...] = jnp.zeros_like(state_ref)

    c = RET_CHUNK
    dh = RET_DIM
    inv_scale = dh ** -0.5
    ri = lax.broadcasted_iota(jnp.int32, (c, c), 0)
    ci = lax.broadcasted_iota(jnp.int32, (c, c), 1)
    rel = (ri - ci).astype(F32)
    pos = lax.broadcasted_iota(jnp.int32, (c, 1), 0).astype(F32)
    decays = []
    for hh in range(RET_HP):
        lg = lg_ref[pl.program_id(0) * RET_HP + hh]
        intra = jnp.where(rel >= 0, jnp.exp(lg * jnp.maximum(rel, 0.0)), 0.0) * inv_scale
        q_decay = jnp.exp(lg * (pos + 1.0))
        k_decay = jnp.exp(lg * (c - 1.0 - pos)) * inv_scale
        chunk_decay = jnp.exp(jnp.full((1, 1), lg * c, F32))
        decays.append((intra, q_decay, k_decay, chunk_decay))

    for t in range(RET_T // c):
        rows = slice(t * c, (t + 1) * c)
        for hh in range(RET_HP):
            intra, q_decay, k_decay, chunk_decay = decays[hh]
            cols = slice(hh * dh, (hh + 1) * dh)
            q = q_ref[rows, cols]
            k = k_ref[rows, cols]
            v = v_ref[rows, cols]
            state = state_ref[hh]
            scores = lax.dot_general(q, k, (((1,), (1,)), ((), ())), preferred_element_type=F32) * intra
            o = _dot(scores.astype(BF16), v)
            o = o + _dot(q, state.astype(BF16)) * q_decay
            kd = (k.astype(F32) * k_decay).astype(BF16)
            kv = lax.dot_general(kd, v, (((0,), (0,)), ((), ())), preferred_element_type=F32)
            state_ref[hh] = state * chunk_decay + kv
            mu = jnp.mean(o, axis=-1, keepdims=True)
            oc = o - mu
            var = jnp.mean(oc * oc, axis=-1, keepdims=True)
            on = oc * lax.rsqrt(var + EPS)
            o_ref[rows, cols] = (on * _silu(g_ref[rows, cols].astype(F32))).astype(o_ref.dtype)


def _retention(p_ret, log_g):
    m = p_ret.shape[0]
    w = RET_HP * RET_DIM
    groups = RET_HEADS // RET_HP

    def spec(part):
        return pl.BlockSpec((RET_T, w), lambda g, s: (s, part * groups + g))

    return pl.pallas_call(
        _ret_kernel,
        grid=(groups, m // RET_T),
        in_specs=[pl.BlockSpec(memory_space=pltpu.SMEM), spec(0), spec(1), spec(2), spec(3)],
        out_specs=pl.BlockSpec((RET_T, w), lambda g, s: (s, g)),
        out_shape=jax.ShapeDtypeStruct((m, RET_W), BF16),
        scratch_shapes=[pltpu.VMEM((RET_HP, RET_DIM, RET_DIM), F32)],
        compiler_params=_params("arbitrary", "arbitrary"),
        name="retention",
    )(log_g, p_ret, p_ret, p_ret, p_ret)


MOBA_KC = 1024
MOBA_FILL = 2048
MOBA_HP = 2
MOBA_BLOCK_SHIFT = MOBA_BLOCK.bit_length() - 1


def _moba_kernel(sl_ref, q_ref, k_ref, v_ref, o_ref, ka_ref, va_ref, km_ref, s_ref, acc_ref):
    i = pl.program_id(1)
    bs = MOBA_BLOCK
    d = MOBA_DIM
    kc = MOBA_KC
    hp = MOBA_HP
    seq = k_ref.shape[0]
    heads = range(hp)

    @pl.when(i == 0)
    def _():
        km_ref[...] = jnp.zeros_like(km_ref)
        fr = MOBA_FILL
        lane = lax.broadcasted_iota(jnp.int32, (fr, d), 1)
        sub = lax.broadcasted_iota(jnp.int32, (d, kc), 0)
        key = lax.broadcasted_iota(jnp.int32, (d, kc), 1)

        def fill(c, carry):
            r = pl.multiple_of(c * fr, fr)
            ones_col = jnp.where(lane == 0, 1.0, 0.0).astype(BF16)
            for hh in heads:
                kb = k_ref[pl.ds(r, fr), hh * d:(hh + 1) * d]
                for t in range(fr // kc):
                    chunk = c * (fr // kc) + t
                    onehot = sub == jnp.right_shift(key + chunk * kc, MOBA_BLOCK_SHIFT)
                    ka_ref[hh, chunk, 0:d, :] = kb[t * kc:(t + 1) * kc, :].astype(F32).T.astype(BF16)
                    ka_ref[hh, chunk, d:2 * d, :] = jnp.where(onehot, 1.0, 0.0).astype(BF16)
                va_ref[hh, pl.ds(r, fr), 0:d] = v_ref[pl.ds(r, fr), hh * d:(hh + 1) * d]
                va_ref[hh, pl.ds(r, fr), d:2 * d] = ones_col
                kmean = jnp.mean(kb.astype(F32).reshape(fr // bs, bs, d), axis=1)
                km_ref[hh, pl.ds(pl.multiple_of(c * (fr // bs), fr // bs), fr // bs), :] = kmean
            return carry

        lax.fori_loop(0, seq // fr, fill, 0)

    scale = d ** -0.5
    slopes = [sl_ref[pl.program_id(0) * hp + hh] for hh in heads]

    nbp = -(-(seq // bs) // 8) * 8
    blk = lax.broadcasted_iota(jnp.int32, (nbp, bs), 0)
    blkf = blk.astype(F32)
    q_augs = []
    for hh in heads:
        q = q_ref[:, hh * d:(hh + 1) * d]
        gate = lax.dot_general(km_ref[hh], q, (((1,), (1,)), ((), ())),
                               precision=lax.Precision.HIGHEST, preferred_element_type=F32)[0:nbp, :]
        g = jnp.where(blk < i, gate, -jnp.inf)
        chosen = blk == i
        for _ in range(MOBA_TOPK):
            mx = jnp.max(g, axis=0, keepdims=True)
            idx = jnp.min(jnp.where(g == mx, blkf, 2.0 * d), axis=0, keepdims=True)
            hit = blkf == idx
            chosen = jnp.logical_or(chosen, jnp.logical_and(hit, mx > -jnp.inf))
            g = jnp.where(hit, -jnp.inf, g)
        bias_t = jnp.concatenate(
            [jnp.where(chosen, 0.0, NEG_BIG), jnp.zeros((d - nbp, bs), F32)], axis=0)
        q_augs.append(jnp.concatenate([q.astype(BF16), bias_t.T.astype(BF16)], axis=1))

    def scores(j, causal, hh):
        r = pl.multiple_of(j * kc, kc)
        s = _dot(q_augs[hh], ka_ref[hh, j]) * scale
        rel = lax.broadcasted_iota(jnp.int32, (1, kc), 1) + (r - i * bs)
        s = s + slopes[hh] * rel.astype(F32)
        if causal:
            row = lax.broadcasted_iota(jnp.int32, (bs, kc), 0)
            s = jnp.where(rel <= row, s, NEG_BIG)
        s_ref[hh, j] = s
        part = s[:, 0:d]
        for t in range(1, kc // d):
            part = jnp.maximum(part, s[:, t * d:(t + 1) * d])
        return part

    def sweep(j, mxs, causal):
        return tuple(jnp.maximum(mxs[hh], scores(j, causal, hh)) for hh in heads)

    n_past = i // (kc // bs)
    mxs = tuple(jnp.full((bs, d), -jnp.inf, F32) for _ in heads)
    mxs = lax.fori_loop(0, n_past, lambda j, c: sweep(j, c, False), mxs)
    mxs = sweep(n_past, mxs, True)
    ms = [jnp.max(mxs[hh], axis=1, keepdims=True) for hh in heads]

    acc_ref[...] = jnp.zeros_like(acc_ref)

    def weighted(j, carry):
        r = pl.multiple_of(j * kc, kc)
        for hh in heads:
            p = jnp.exp(s_ref[hh, j] - ms[hh]).astype(BF16)
            acc_ref[hh] += _dot(p, va_ref[hh, pl.ds(r, kc), :])
        return carry

    lax.fori_loop(0, n_past + 1, weighted, 0)
    for hh in heads:
        acc = acc_ref[hh]
        o_ref[:, hh * d:(hh + 1) * d] = (acc[:, 0:d] / acc[:, d:d + 1]).astype(o_ref.dtype)


def _moba(q_all, kv_all, slopes):
    m = q_all.shape[0]
    hp = MOBA_HP
    w = hp * MOBA_DIM
    groups = q_all.shape[1] // w
    staged_keys = pltpu.VMEM((hp, m // MOBA_KC, 2 * MOBA_DIM, MOBA_KC), BF16)
    staged_values = pltpu.VMEM((hp, m, 2 * MOBA_DIM), BF16)
    return pl.pallas_call(
        _moba_kernel,
        grid=(groups, m // MOBA_BLOCK),
        in_specs=[
            pl.BlockSpec(memory_space=pltpu.SMEM),
            pl.BlockSpec((MOBA_BLOCK, w), lambda g, i: (i, g)),
            pl.BlockSpec((m, w), lambda g, i: (0, g)),
            pl.BlockSpec((m, w), lambda g, i: (0, groups + g)),
        ],
        out_specs=pl.BlockSpec((MOBA_BLOCK, w), lambda g, i: (i, g)),
        out_shape=jax.ShapeDtypeStruct(q_all.shape, BF16),
        scratch_shapes=[
            staged_keys,
            staged_values,
            pltpu.VMEM((hp, MOBA_DIM, MOBA_DIM), F32),
            pltpu.VMEM((hp, m // MOBA_KC, MOBA_BLOCK, MOBA_KC), F32),
            pltpu.VMEM((hp, MOBA_BLOCK, 2 * MOBA_DIM), F32),
        ],
        compiler_params=_params("arbitrary", "arbitrary"),
        name="moba",
    )(slopes, q_all, kv_all, kv_all)


def _split_mod(mod):
    d = D_MODEL
    return mod[:, :d], mod[:, d:2 * d], mod[:, 2 * d:]


def _ffn(x, c_col, norm_g, mod_w, mod_b, w1, w3, w2):
    shift, scale, gate = _split_mod(_mod_vector(c_col, mod_w, mod_b))
    h = _norm_modulate(x, norm_g, scale, shift)
    u = _swiglu_up(h, w1, w3)
    return _project_residual(u, w2.astype(BF16), x, gate, tm=512, name="ffn_down")


def kernel(x, c, l0_mix_norm_g, l0_mix_mod_w, l0_mix_mod_b, l0_mix_w_in, l0_mix_w_out, l0_ffn_norm_g, l0_ffn_mod_w, l0_ffn_mod_b, l0_ffn_w1, l0_ffn_w3, l0_ffn_w2, l1_mix_norm_g, l1_mix_mod_w, l1_mix_mod_b, l1_mix_w_in, l1_mix_conv_w, l1_mix_w_out, l1_ffn_norm_g, l1_ffn_mod_w, l1_ffn_mod_b, l1_ffn_w1, l1_ffn_w3, l1_ffn_w2, final_norm_g):
    b, seq, d = x.shape
    xs = x.reshape(b * seq, d)
    c_col = c.reshape(d, 1)

    log_g = jnp.log1p(-jnp.exp2(-5.0 - jnp.arange(RET_HEADS, dtype=F32)))
    slopes = jnp.exp2(-8.0 * jnp.arange(1, MOBA_HEADS + 1, dtype=F32) / MOBA_HEADS)

    shift, scale, gate = _split_mod(_mod_vector(c_col, l0_mix_mod_w, l0_mix_mod_b))
    h = _norm_modulate(xs, l0_mix_norm_g, scale, shift)
    w_in = l0_mix_w_in
    p_ret = _project(h, w_in, 0, 4 * RET_W, BF16, name="mix0_in_ret")
    moba_q = _project(h, w_in, 4 * RET_W, MOBA_W, F32, name="mix0_in_q")
    moba_kv = _project(h, w_in, 4 * RET_W + MOBA_W, 2 * MOBA_W, BF16, name="mix0_in_kv")
    ret = _retention(p_ret, log_g)
    att = _moba(moba_q, moba_kv, slopes)
    xs = _project_residual_cat(ret, att, l0_mix_w_out, xs, gate, name="mix0_out")
    xs = _ffn(xs, c_col, l0_ffn_norm_g, l0_ffn_mod_w, l0_ffn_mod_b, l0_ffn_w1, l0_ffn_w3, l0_ffn_w2)

    shift, scale, gate = _split_mod(_mod_vector(c_col, l1_mix_mod_w, l1_mix_mod_b))
    h = _norm_modulate(xs, l1_mix_norm_g, scale, shift)
    s = _conv_project(h, l1_mix_w_in, l1_mix_conv_w.reshape(3, d))
    xs = _project_residual(s, l1_mix_w_out, xs, gate, name="mix1_out")
    xs = _ffn(xs, c_col, l1_ffn_norm_g, l1_ffn_mod_w, l1_ffn_mod_b, l1_ffn_w1, l1_ffn_w3, l1_ffn_w2)

    return _final_norm(xs, final_norm_g).reshape(b, seq, d)
```

```python
import functools

import jax
import jax.numpy as jnp
from jax import lax
from jax.experimental import pallas as pl
from jax.experimental.pallas import tpu as pltpu

D_MODEL = 4096
SEQ = 8192
RET_HEADS = 8
RET_DIM = 256
RET_CHUNK = 128
MOBA_HEADS = 16
MOBA_DIM = 128
MOBA_BLOCK = 256
MOBA_TOPK = 3
FFN_HIDDEN = 11008
EPS = 1e-6
RET_W = RET_HEADS * RET_DIM
MOBA_W = MOBA_HEADS * MOBA_DIM
N_BLOCKS = SEQ // MOBA_BLOCK

VMEM_LIMIT_BYTES = 56 * 1024 * 1024
NEG_BIG = -1e30

BF16 = jnp.bfloat16
F32 = jnp.float32


def _params(*sem):
    return pltpu.CompilerParams(dimension_semantics=sem, vmem_limit_bytes=VMEM_LIMIT_BYTES)


def _silu(v):
    return v * jax.nn.sigmoid(v)


MOD_TN = 512
MOD_KC = 128


def _mod_kernel(c_ref, w_ref, b_ref, o_ref, ca_ref):
    lanes = ca_ref.shape[1]

    @pl.when(pl.program_id(0) == 0)
    def _():
        ca_ref[...] = jnp.broadcast_to(_silu(c_ref[...]), ca_ref.shape)

    def body(k, acc):
        r = pl.multiple_of(k * MOD_KC, MOD_KC)
        ca = ca_ref[pl.ds(r, MOD_KC), :]
        p = w_ref[pl.ds(r, MOD_KC), :] * jnp.concatenate([ca] * (MOD_TN // lanes), axis=1)
        return acc + jnp.sum(p.reshape(MOD_KC // 8, 8, MOD_TN), axis=0)

    acc = lax.fori_loop(0, D_MODEL // MOD_KC, body, jnp.zeros((8, MOD_TN), F32), unroll=4)
    o_ref[...] = jnp.sum(acc, axis=0, keepdims=True) + b_ref[...]


def _mod_vector(c_col, w, b):
    n = w.shape[1]
    return pl.pallas_call(
        _mod_kernel,
        grid=(n // MOD_TN,),
        in_specs=[
            pl.BlockSpec((D_MODEL, 1), lambda j: (0, 0)),
            pl.BlockSpec((D_MODEL, MOD_TN), lambda j: (0, j)),
            pl.BlockSpec((1, MOD_TN), lambda j: (0, j)),
        ],
        out_specs=pl.BlockSpec((1, MOD_TN), lambda j: (0, j)),
        out_shape=jax.ShapeDtypeStruct((1, n), F32),
        scratch_shapes=[pltpu.VMEM((D_MODEL, 128), F32)],
        compiler_params=_params("arbitrary"),
        name="adaln_mod",
    )(c_col, w, b.reshape(1, n))


NORM_TM = 256


NORM_RC = 16


def _norm_rows(x_ref, o_ref, mult, add):
    def body(r, carry):
        rows = pl.ds(pl.multiple_of(r * NORM_RC, NORM_RC), NORM_RC)
        x = x_ref[rows, :]
        ms = jnp.mean(x * x, axis=-1, keepdims=True)
        y = x * lax.rsqrt(ms + EPS) * mult
        if add is not None:
            y = y + add
        o_ref[rows, :] = y.astype(o_ref.dtype)
        return carry

    lax.fori_loop(0, x_ref.shape[0] // NORM_RC, body, 0, unroll=8)


def _normmod_kernel(x_ref, g_ref, sc_ref, sh_ref, o_ref):
    _norm_rows(x_ref, o_ref, g_ref[...] * (1.0 + sc_ref[...]), sh_ref[...])


def _norm_kernel(x_ref, g_ref, o_ref):
    _norm_rows(x_ref, o_ref, g_ref[...], None)


def _norm_modulate(x, g, scale, shift):
    m, d = x.shape
    row = pl.BlockSpec((NORM_TM, d), lambda i: (i, 0))
    vec = pl.BlockSpec((1, d), lambda i: (0, 0))
    return pl.pallas_call(
        _normmod_kernel,
        grid=(m // NORM_TM,),
        in_specs=[row, vec, vec, vec],
        out_specs=row,
        out_shape=jax.ShapeDtypeStruct((m, d), BF16),
        compiler_params=_params("arbitrary"),
        name="norm_modulate",
    )(x, g.reshape(1, d), scale, shift)


def _final_norm(x, g):
    m, d = x.shape
    row = pl.BlockSpec((NORM_TM, d), lambda i: (i, 0))
    vec = pl.BlockSpec((1, d), lambda i: (0, 0))
    return pl.pallas_call(
        _norm_kernel,
        grid=(m // NORM_TM,),
        in_specs=[row, vec],
        out_specs=row,
        out_shape=jax.ShapeDtypeStruct((m, d), F32),
        compiler_params=_params("arbitrary"),
        name="final_norm",
    )(x, g.reshape(1, d))


def _dot(a, b):
    return jnp.dot(a, b.astype(BF16), preferred_element_type=F32)


def _proj_kernel(a_ref, w_ref, o_ref):
    o_ref[...] = _dot(a_ref[...], w_ref[...]).astype(o_ref.dtype)


def _project(a, w, col0, ncols, out_dtype, tm=1024, tn=512, name="proj"):
    m, k = a.shape
    cb = col0 // tn
    return pl.pallas_call(
        _proj_kernel,
        grid=(m // tm, ncols // tn),
        in_specs=[
            pl.BlockSpec((tm, k), lambda i, j: (i, 0)),
            pl.BlockSpec((k, tn), lambda i, j: (0, cb + j)),
        ],
        out_specs=pl.BlockSpec((tm, tn), lambda i, j: (i, j)),
        out_shape=jax.ShapeDtypeStruct((m, ncols), out_dtype),
        compiler_params=_params("arbitrary", "arbitrary"),
        name=name,
    )(a, w)


def _swiglu_kernel(a_ref, w1_ref, w3_ref, o_ref):
    a = a_ref[...]
    h1 = _dot(a, w1_ref[...])
    h3 = _dot(a, w3_ref[...])
    o_ref[...] = (_silu(h1) * h3).astype(o_ref.dtype)


def _swiglu_up(a, w1, w3, tm=2048, tn=256):
    m, k = a.shape
    n = w1.shape[1]
    wspec = pl.BlockSpec((k, tn), lambda i, j: (0, j))
    return pl.pallas_call(
        _swiglu_kernel,
        grid=(m // tm, n // tn),
        in_specs=[pl.BlockSpec((tm, k), lambda i, j: (i, 0)), wspec, wspec],
        out_specs=pl.BlockSpec((tm, tn), lambda i, j: (i, j)),
        out_shape=jax.ShapeDtypeStruct((m, n), BF16),
        compiler_params=_params("arbitrary", "arbitrary"),
        name="swiglu_up",
    )(a, w1, w3)


def _resid_kernel(a_ref, w_ref, x_ref, g_ref, o_ref, *, nk):
    part = g_ref[...] * _dot(a_ref[...], w_ref[...])
    if nk == 1:
        o_ref[...] = x_ref[...] + part
        return

    @pl.when(pl.program_id(2) == 0)
    def _():
        o_ref[...] = x_ref[...]

    o_ref[...] += part


def _project_residual(a, w, x, gate, tm=1024, tn=512, tk=None, name="proj_resid"):
    m, k = a.shape
    n = w.shape[1]
    tk = k if tk is None else tk
    nk = k // tk
    return pl.pallas_call(
        functools.partial(_resid_kernel, nk=nk),
        grid=(m // tm, n // tn, nk),
        in_specs=[
            pl.BlockSpec((tm, tk), lambda i, j, kk: (i, kk)),
            pl.BlockSpec((tk, tn), lambda i, j, kk: (kk, j)),
            pl.BlockSpec((tm, tn), lambda i, j, kk: (i, j)),
            pl.BlockSpec((1, tn), lambda i, j, kk: (0, j)),
        ],
        out_specs=pl.BlockSpec((tm, tn), lambda i, j, kk: (i, j)),
        out_shape=jax.ShapeDtypeStruct((m, n), F32),
        compiler_params=_params("arbitrary", "arbitrary", "arbitrary"),
        name=name,
    )(a, w, x, gate)


def _resid_cat_kernel(a1_ref, a2_ref, w_ref, x_ref, g_ref, o_ref):
    k1 = a1_ref.shape[1]
    part = _dot(a1_ref[...], w_ref[0:k1, :]) + _dot(a2_ref[...], w_ref[k1:, :])
    o_ref[...] = x_ref[...] + g_ref[...] * part


def _project_residual_cat(a1, a2, w, x, gate, tm=1024, tn=512, name="proj_resid_cat"):
    m, k1 = a1.shape
    k2 = a2.shape[1]
    n = w.shape[1]
    return pl.pallas_call(
        _resid_cat_kernel,
        grid=(m // tm, n // tn),
        in_specs=[
            pl.BlockSpec((tm, k1), lambda i, j: (i, 0)),
            pl.BlockSpec((tm, k2), lambda i, j: (i, 0)),
            pl.BlockSpec((k1 + k2, tn), lambda i, j: (0, j)),
            pl.BlockSpec((tm, tn), lambda i, j: (i, j)),
            pl.BlockSpec((1, tn), lambda i, j: (0, j)),
        ],
        out_specs=pl.BlockSpec((tm, tn), lambda i, j: (i, j)),
        out_shape=jax.ShapeDtypeStruct((m, n), F32),
        compiler_params=_params("arbitrary", "arbitrary"),
        name=name,
    )(a1, a2, w, x, gate)


CONV_HALO = 16


def _conv_proj_kernel(a_ref, ap_ref, wb_ref, wc_ref, wu_ref, cw_ref, o_ref, ax_ref):
    i = pl.program_id(0)
    h = CONV_HALO

    @pl.when(pl.program_id(1) == 0)
    def _():
        ax_ref[0:h, :] = ap_ref[...]
        ax_ref[h:, :] = a_ref[...]

    ax = ax_ref[...]
    b = _dot(a_ref[...], wb_ref[...])
    z = _dot(ax, wc_ref[...]) * _dot(ax, wu_ref[...])
    row = lax.broadcasted_iota(jnp.int32, z.shape, 0)
    z = jnp.where(jnp.logical_or(i > 0, row >= h), z, 0.0)
    z1 = pltpu.roll(z, 1, 0)
    z2 = pltpu.roll(z, 2, 0)
    cw = cw_ref[...]
    y = cw[2:3, :] * z[h:, :] + cw[1:2, :] * z1[h:, :] + cw[0:1, :] * z2[h:, :]
    o_ref[...] = (b * y).astype(o_ref.dtype)


def _conv_project(a, w, conv_w, tm=1024, tn=256):
    m, k = a.shape
    n = D_MODEL
    nb = n // tn
    hb = tm // CONV_HALO
    return pl.pallas_call(
        _conv_proj_kernel,
        grid=(m // tm, nb),
        in_specs=[
            pl.BlockSpec((tm, k), lambda i, j: (i, 0), pipeline_mode=pl.Buffered(1)),
            pl.BlockSpec((CONV_HALO, k), lambda i, j: (jnp.maximum(i * hb - 1, 0), 0)),
            pl.BlockSpec((k, tn), lambda i, j: (0, j)),
            pl.BlockSpec((k, tn), lambda i, j: (0, nb + j)),
            pl.BlockSpec((k, tn), lambda i, j: (0, 2 * nb + j)),
            pl.BlockSpec((3, tn), lambda i, j: (0, j)),
        ],
        out_specs=pl.BlockSpec((tm, tn), lambda i, j: (i, j)),
        out_shape=jax.ShapeDtypeStruct((m, n), BF16),
        scratch_shapes=[pltpu.VMEM((tm + CONV_HALO, k), BF16)],
        compiler_params=_params("arbitrary", "arbitrary"),
        name="conv_proj",
    )(a, a, w, w, w, conv_w)


RET_T = 512
RET_HP = 2


def _ret_kernel(lg_ref, q_ref, k_ref, v_ref, g_ref, *refs, n_cast):
    for src_ref, dst_ref in zip(refs[:n_cast], refs[n_cast + 1:2 * n_cast + 1]):
        dst_ref[...] = src_ref[...].astype(dst_ref.dtype)
    o_ref, state_ref = refs[n_cast], refs[2 * n_cast + 1]

    @pl.when(pl.program_id(1) == 0)
    def _():
        state_ref[...] = jnp.zeros_like(state_ref)

    c = RET_CHUNK
    dh = RET_DIM
    inv_scale = dh ** -0.5
    ri = lax.broadcasted_iota(jnp.int32, (c, c), 0)
    ci = lax.broadcasted_iota(jnp.int32, (c, c), 1)
    rel = (ri - ci).astype(F32)
    pos = lax.broadcasted_iota(jnp.int32, (c, 1), 0).astype(F32)
    decays = []
    for hh in range(RET_HP):
        lg = lg_ref[pl.program_id(0) * RET_HP + hh]
        intra = jnp.where(rel >= 0, jnp.exp(lg * jnp.maximum(rel, 0.0)), 0.0) * inv_scale
        q_decay = jnp.exp(lg * (pos + 1.0))
        k_decay = jnp.exp(lg * (c - 1.0 - pos)) * inv_scale
        chunk_decay = jnp.exp(jnp.full((1, 1), lg * c, F32))
        decays.append((intra, q_decay, k_decay, chunk_decay))

    for t in range(RET_T // c):
        rows = slice(t * c, (t + 1) * c)
        for hh in range(RET_HP):
            intra, q_decay, k_decay, chunk_decay = decays[hh]
            cols = slice(hh * dh, (hh + 1) * dh)
            q = q_ref[rows, cols]
            k = k_ref[rows, cols]
            v = v_ref[rows, cols]
            state = state_ref[hh]
            scores = lax.dot_general(q, k, (((1,), (1,)), ((), ())), preferred_element_type=F32) * intra
            o = _dot(scores.astype(BF16), v)
            o = o + _dot(q, state.astype(BF16)) * q_decay
            kd = (k.astype(F32) * k_decay).astype(BF16)
            kv = lax.dot_general(kd, v, (((0,), (0,)), ((), ())), preferred_element_type=F32)
            state_ref[hh] = state * chunk_decay + kv
            mu = jnp.mean(o, axis=-1, keepdims=True)
            oc = o - mu
            var = jnp.mean(oc * oc, axis=-1, keepdims=True)
            on = oc * lax.rsqrt(var + EPS)
            o_ref[rows, cols] = (on * _silu(g_ref[rows, cols].astype(F32))).astype(o_ref.dtype)


RET_CAST_COLS = 4


def _retention(p_ret, log_g, cast_ws=()):
    m = p_ret.shape[0]
    w = RET_HP * RET_DIM
    groups = RET_HEADS // RET_HP
    ns = m // RET_T
    row_blocks = groups * ns // RET_CAST_COLS

    def spec(part):
        return pl.BlockSpec((RET_T, w), lambda g, s: (s, part * groups + g))

    def cast_block(g, s):
        step = g * ns + s
        return (step // RET_CAST_COLS, step % RET_CAST_COLS)

    cast_specs = [pl.BlockSpec((cw.shape[0] // row_blocks, cw.shape[1] // RET_CAST_COLS), cast_block)
                  for cw in cast_ws]
    outs = pl.pallas_call(
        functools.partial(_ret_kernel, n_cast=len(cast_ws)),
        grid=(groups, ns),
        in_specs=[pl.BlockSpec(memory_space=pltpu.SMEM), spec(0), spec(1), spec(2), spec(3)] + cast_specs,
        out_specs=[pl.BlockSpec((RET_T, w), lambda g, s: (s, g))] + cast_specs,
        out_shape=[jax.ShapeDtypeStruct((m, RET_W), BF16)]
        + [jax.ShapeDtypeStruct(cw.shape, BF16) for cw in cast_ws],
        scratch_shapes=[pltpu.VMEM((RET_HP, RET_DIM, RET_DIM), F32)],
        compiler_params=_params("arbitrary", "arbitrary"),
        name="retention",
    )(log_g, p_ret, p_ret, p_ret, p_ret, *cast_ws)
    return outs[0], outs[1:]


MOBA_KC = 1024
MOBA_FILL = 2048
MOBA_HP = 2
MOBA_BLOCK_SHIFT = MOBA_BLOCK.bit_length() - 1


def _moba_kernel(sl_ref, q_ref, k_ref, v_ref, o_ref, ka_ref, va_ref, km_ref, s_ref, acc_ref):
    i = pl.program_id(1)
    bs = MOBA_BLOCK
    d = MOBA_DIM
    kc = MOBA_KC
    hp = MOBA_HP
    seq = k_ref.shape[0]
    heads = range(hp)

    @pl.when(i == 0)
    def _():
        km_ref[...] = jnp.zeros_like(km_ref)
        fr = MOBA_FILL
        lane = lax.broadcasted_iota(jnp.int32, (fr, d), 1)
        sub = lax.broadcasted_iota(jnp.int32, (d, kc), 0)
        key = lax.broadcasted_iota(jnp.int32, (d, kc), 1)

        def fill(c, carry):
            r = pl.multiple_of(c * fr, fr)
            ones_col = jnp.where(lane == 0, 1.0, 0.0).astype(BF16)
            for hh in heads:
                kb = k_ref[pl.ds(r, fr), hh * d:(hh + 1) * d]
                for t in range(fr // kc):
                    chunk = c * (fr // kc) + t
                    onehot = sub == jnp.right_shift(key + chunk * kc, MOBA_BLOCK_SHIFT)
                    ka_ref[hh, chunk, 0:d, :] = kb[t * kc:(t + 1) * kc, :].astype(F32).T.astype(BF16)
                    ka_ref[hh, chunk, d:2 * d, :] = jnp.where(onehot, 1.0, 0.0).astype(BF16)
                va_ref[hh, pl.ds(r, fr), 0:d] = v_ref[pl.ds(r, fr), hh * d:(hh + 1) * d]
                va_ref[hh, pl.ds(r, fr), d:2 * d] = ones_col
                kmean = jnp.mean(kb.astype(F32).reshape(fr // bs, bs, d), axis=1)
                km_ref[hh, pl.ds(pl.multiple_of(c * (fr // bs), fr // bs), fr // bs), :] = kmean
            return carry

        lax.fori_loop(0, seq // fr, fill, 0)

    scale = d ** -0.5
    slopes = [sl_ref[pl.program_id(0) * hp + hh] for hh in heads]

    nbp = -(-(seq // bs) // 8) * 8
    blk = lax.broadcasted_iota(jnp.int32, (nbp, bs), 0)
    blkf = blk.astype(F32)
    q_augs = []
    for hh in heads:
        q = q_ref[:, hh * d:(hh + 1) * d]
        gate = lax.dot_general(km_ref[hh], q, (((1,), (1,)), ((), ())),
                               precision=lax.Precision.HIGHEST, preferred_element_type=F32)[0:nbp, :]
        g = jnp.where(blk < i, gate, -jnp.inf)
        chosen = blk == i
        for _ in range(MOBA_TOPK):
            mx = jnp.max(g, axis=0, keepdims=True)
            idx = jnp.min(jnp.where(g == mx, blkf, 2.0 * d), axis=0, keepdims=True)
            hit = blkf == idx
            chosen = jnp.logical_or(chosen, jnp.logical_and(hit, mx > -jnp.inf))
            g = jnp.where(hit, -jnp.inf, g)
        bias_t = jnp.concatenate(
            [jnp.where(chosen, 0.0, NEG_BIG), jnp.zeros((d - nbp, bs), F32)], axis=0)
        q_augs.append(jnp.concatenate([q.astype(BF16), bias_t.T.astype(BF16)], axis=1))

    def scores(j, causal, hh):
        r = pl.multiple_of(j * kc, kc)
        s = _dot(q_augs[hh], ka_ref[hh, j]) * scale
        rel = lax.broadcasted_iota(jnp.int32, (1, kc), 1) + (r - i * bs)
        s = s + slopes[hh] * rel.astype(F32)
        if causal:
            row = lax.broadcasted_iota(jnp.int32, (bs, kc), 0)
            s = jnp.where(rel <= row, s, NEG_BIG)
        s_ref[hh, j] = s
        part = s[:, 0:d]
        for t in range(1, kc // d):
            part = jnp.maximum(part, s[:, t * d:(t + 1) * d])
        return part

    def sweep(j, mxs, causal):
        return tuple(jnp.maximum(mxs[hh], scores(j, causal, hh)) for hh in heads)

    n_past = i // (kc // bs)
    mxs = tuple(jnp.full((bs, d), -jnp.inf, F32) for _ in heads)
    mxs = lax.fori_loop(0, n_past, lambda j, c: sweep(j, c, False), mxs)
    mxs = sweep(n_past, mxs, True)
    ms = [jnp.max(mxs[hh], axis=1, keepdims=True) for hh in heads]

    acc_ref[...] = jnp.zeros_like(acc_ref)

    def weighted(j, carry):
        r = pl.multiple_of(j * kc, kc)
        for hh in heads:
            p = jnp.exp(s_ref[hh, j] - ms[hh]).astype(BF16)
            acc_ref[hh] += _dot(p, va_ref[hh, pl.ds(r, kc), :])
        return carry

    lax.fori_loop(0, n_past + 1, weighted, 0)
    for hh in heads:
        acc = acc_ref[hh]
        o_ref[:, hh * d:(hh + 1) * d] = (acc[:, 0:d] / acc[:, d:d + 1]).astype(o_ref.dtype)


def _moba(q_all, kv_all, slopes):
    m = q_all.shape[0]
    hp = MOBA_HP
    w = hp * MOBA_DIM
    groups = q_all.shape[1] // w
    staged_keys = pltpu.VMEM((hp, m // MOBA_KC, 2 * MOBA_DIM, MOBA_KC), BF16)
    staged_values = pltpu.VMEM((hp, m, 2 * MOBA_DIM), BF16)
    return pl.pallas_call(
        _moba_kernel,
        grid=(groups, m // MOBA_BLOCK),
        in_specs=[
            pl.BlockSpec(memory_space=pltpu.SMEM),
            pl.BlockSpec((MOBA_BLOCK, w), lambda g, i: (i, g)),
            pl.BlockSpec((m, w), lambda g, i: (0, g)),
            pl.BlockSpec((m, w), lambda g, i: (0, groups + g)),
        ],
        out_specs=pl.BlockSpec((MOBA_BLOCK, w), lambda g, i: (i, g)),
        out_shape=jax.ShapeDtypeStruct(q_all.shape, BF16),
        scratch_shapes=[
            staged_keys,
            staged_values,
            pltpu.VMEM((hp, MOBA_DIM, MOBA_DIM), F32),
            pltpu.VMEM((hp, m // MOBA_KC, MOBA_BLOCK, MOBA_KC), F32),
            pltpu.VMEM((hp, MOBA_BLOCK, 2 * MOBA_DIM), F32),
        ],
        compiler_params=_params("arbitrary", "arbitrary"),
        name="moba",
    )(slopes, q_all, kv_all, kv_all)


def _split_mod(mod):
    d = D_MODEL
    return mod[:, :d], mod[:, d:2 * d], mod[:, 2 * d:]


def _ffn(x, c_col, norm_g, mod_w, mod_b, w1, w3, w2_bf16):
    shift, scale, gate = _split_mod(_mod_vector(c_col, mod_w, mod_b))
    h = _norm_modulate(x, norm_g, scale, shift)
    u = _swiglu_up(h, w1, w3)
    return _project_residual(u, w2_bf16, x, gate, tm=512, name="ffn_down")


def kernel(x, c, l0_mix_norm_g, l0_mix_mod_w, l0_mix_mod_b, l0_mix_w_in, l0_mix_w_out, l0_ffn_norm_g, l0_ffn_mod_w, l0_ffn_mod_b, l0_ffn_w1, l0_ffn_w3, l0_ffn_w2, l1_mix_norm_g, l1_mix_mod_w, l1_mix_mod_b, l1_mix_w_in, l1_mix_conv_w, l1_mix_w_out, l1_ffn_norm_g, l1_ffn_mod_w, l1_ffn_mod_b, l1_ffn_w1, l1_ffn_w3, l1_ffn_w2, final_norm_g):
    b, seq, d = x.shape
    xs = x.reshape(b * seq, d)
    c_col = c.reshape(d, 1)

    log_g = jnp.log1p(-jnp.exp2(-5.0 - jnp.arange(RET_HEADS, dtype=F32)))
    slopes = jnp.exp2(-8.0 * jnp.arange(1, MOBA_HEADS + 1, dtype=F32) / MOBA_HEADS)

    shift, scale, gate = _split_mod(_mod_vector(c_col, l0_mix_mod_w, l0_mix_mod_b))
    h = _norm_modulate(xs, l0_mix_norm_g, scale, shift)
    w_in = l0_mix_w_in
    p_ret = _project(h, w_in, 0, 4 * RET_W, BF16, name="mix0_in_ret")
    moba_q = _project(h, w_in, 4 * RET_W, MOBA_W, F32, name="mix0_in_q")
    moba_kv = _project(h, w_in, 4 * RET_W + MOBA_W, 2 * MOBA_W, BF16, name="mix0_in_kv")
    ret, (l0_w2, l1_w2) = _retention(p_ret, log_g, (l0_ffn_w2, l1_ffn_w2))
    att = _moba(moba_q, moba_kv, slopes)
    xs = _project_residual_cat(ret, att, l0_mix_w_out, xs, gate, name="mix0_out")
    xs = _ffn(xs, c_col, l0_ffn_norm_g, l0_ffn_mod_w, l0_ffn_mod_b, l0_ffn_w1, l0_ffn_w3, l0_w2)

    shift, scale, gate = _split_mod(_mod_vector(c_col, l1_mix_mod_w, l1_mix_mod_b))
    h = _norm_modulate(xs, l1_mix_norm_g, scale, shift)
    s = _conv_project(h, l1_mix_w_in, l1_mix_conv_w.reshape(3, d))
    xs = _project_residual(s, l1_mix_w_out, xs, gate, name="mix1_out")
    xs = _ffn(xs, c_col, l1_ffn_norm_g, l1_ffn_mod_w, l1_ffn_mod_b, l1_ffn_w1, l1_ffn_w3, l1_w2)

    return _final_norm(xs, final_norm_g).reshape(b, seq, d)
```

```python
import functools

import jax
import jax.numpy as jnp
from jax import lax
from jax.experimental import pallas as pl
from jax.experimental.pallas import tpu as pltpu

D_MODEL = 4096
SEQ = 8192
RET_HEADS = 8
RET_DIM = 256
RET_CHUNK = 128
MOBA_HEADS = 16
MOBA_DIM = 128
MOBA_BLOCK = 256
MOBA_TOPK = 3
FFN_HIDDEN = 11008
EPS = 1e-6
RET_W = RET_HEADS * RET_DIM
MOBA_W = MOBA_HEADS * MOBA_DIM
N_BLOCKS = SEQ // MOBA_BLOCK

VMEM_LIMIT_BYTES = 56 * 1024 * 1024
NEG_BIG = -1e30

BF16 = jnp.bfloat16
F32 = jnp.float32


def _params(*sem):
    return pltpu.CompilerParams(dimension_semantics=sem, vmem_limit_bytes=VMEM_LIMIT_BYTES)


def _silu(v):
    return v * jax.nn.sigmoid(v)


MOD_TN = 512
MOD_KC = 128


def _mod_kernel(c_ref, w_ref, b_ref, o_ref, ca_ref):
    lanes = ca_ref.shape[1]

    @pl.when(pl.program_id(0) == 0)
    def _():
        ca_ref[...] = jnp.broadcast_to(_silu(c_ref[...]), ca_ref.shape)

    def body(k, acc):
        r = pl.multiple_of(k * MOD_KC, MOD_KC)
        ca = ca_ref[pl.ds(r, MOD_KC), :]
        p = w_ref[pl.ds(r, MOD_KC), :] * jnp.concatenate([ca] * (MOD_TN // lanes), axis=1)
        return acc + jnp.sum(p.reshape(MOD_KC // 8, 8, MOD_TN), axis=0)

    acc = lax.fori_loop(0, D_MODEL // MOD_KC, body, jnp.zeros((8, MOD_TN), F32), unroll=4)
    o_ref[...] = jnp.sum(acc, axis=0, keepdims=True) + b_ref[...]


def _mod_vector(c_col, w, b):
    n = w.shape[1]
    return pl.pallas_call(
        _mod_kernel,
        grid=(n // MOD_TN,),
        in_specs=[
            pl.BlockSpec((D_MODEL, 1), lambda j: (0, 0)),
            pl.BlockSpec((D_MODEL, MOD_TN), lambda j: (0, j)),
            pl.BlockSpec((1, MOD_TN), lambda j: (0, j)),
        ],
        out_specs=pl.BlockSpec((1, MOD_TN), lambda j: (0, j)),
        out_shape=jax.ShapeDtypeStruct((1, n), F32),
        scratch_shapes=[pltpu.VMEM((D_MODEL, 128), F32)],
        compiler_params=_params("arbitrary"),
        name="adaln_mod",
    )(c_col, w, b.reshape(1, n))


NORM_TM = 256


NORM_RC = 16


def _norm_rows(x_ref, o_ref, mult, add):
    def body(r, carry):
        rows = pl.ds(pl.multiple_of(r * NORM_RC, NORM_RC), NORM_RC)
        x = x_ref[rows, :]
        ms = jnp.mean(x * x, axis=-1, keepdims=True)
        y = x * lax.rsqrt(ms + EPS) * mult
        if add is not None:
            y = y + add
        o_ref[rows, :] = y.astype(o_ref.dtype)
        return carry

    lax.fori_loop(0, x_ref.shape[0] // NORM_RC, body, 0, unroll=8)


def _normmod_kernel(x_ref, g_ref, sc_ref, sh_ref, o_ref):
    _norm_rows(x_ref, o_ref, g_ref[...] * (1.0 + sc_ref[...]), sh_ref[...])


def _norm_kernel(x_ref, g_ref, o_ref):
    _norm_rows(x_ref, o_ref, g_ref[...], None)


def _norm_modulate(x, g, scale, shift):
    m, d = x.shape
    row = pl.BlockSpec((NORM_TM, d), lambda i: (i, 0))
    vec = pl.BlockSpec((1, d), lambda i: (0, 0))
    return pl.pallas_call(
        _normmod_kernel,
        grid=(m // NORM_TM,),
        in_specs=[row, vec, vec, vec],
        out_specs=row,
        out_shape=jax.ShapeDtypeStruct((m, d), BF16),
        compiler_params=_params("arbitrary"),
        name="norm_modulate",
    )(x, g.reshape(1, d), scale, shift)


def _final_norm(x, g):
    m, d = x.shape
    row = pl.BlockSpec((NORM_TM, d), lambda i: (i, 0))
    vec = pl.BlockSpec((1, d), lambda i: (0, 0))
    return pl.pallas_call(
        _norm_kernel,
        grid=(m // NORM_TM,),
        in_specs=[row, vec],
        out_specs=row,
        out_shape=jax.ShapeDtypeStruct((m, d), F32),
        compiler_params=_params("arbitrary"),
        name="final_norm",
    )(x, g.reshape(1, d))


def _dot(a, b):
    return jnp.dot(a, b.astype(BF16), preferred_element_type=F32)


def _proj_kernel(a_ref, w_ref, o_ref):
    o_ref[...] = _dot(a_ref[...], w_ref[...]).astype(o_ref.dtype)


def _project(a, w, col0, ncols, out_dtype, tm=1024, tn=512, name="proj"):
    m, k = a.shape
    cb = col0 // tn
    return pl.pallas_call(
        _proj_kernel,
        grid=(m // tm, ncols // tn),
        in_specs=[
            pl.BlockSpec((tm, k), lambda i, j: (i, 0)),
            pl.BlockSpec((k, tn), lambda i, j: (0, cb + j)),
        ],
        out_specs=pl.BlockSpec((tm, tn), lambda i, j: (i, j)),
        out_shape=jax.ShapeDtypeStruct((m, ncols), out_dtype),
        compiler_params=_params("arbitrary", "arbitrary"),
        name=name,
    )(a, w)


def _swiglu_kernel(a_ref, w1_ref, w3_ref, o_ref):
    a = a_ref[...]
    h1 = _dot(a, w1_ref[...])
    h3 = _dot(a, w3_ref[...])
    o_ref[...] = (_silu(h1) * h3).astype(o_ref.dtype)


def _swiglu_up(a, w1, w3, tm=2048, tn=256):
    m, k = a.shape
    n = w1.shape[1]
    wspec = pl.BlockSpec((k, tn), lambda i, j: (0, j))
    return pl.pallas_call(
        _swiglu_kernel,
        grid=(m // tm, n // tn),
        in_specs=[pl.BlockSpec((tm, k), lambda i, j: (i, 0)), wspec, wspec],
        out_specs=pl.BlockSpec((tm, tn), lambda i, j: (i, j)),
        out_shape=jax.ShapeDtypeStruct((m, n), BF16),
        compiler_params=_params("arbitrary", "arbitrary"),
        name="swiglu_up",
    )(a, w1, w3)


def _resid_kernel(a_ref, w_ref, x_ref, g_ref, o_ref, *, nk):
    part = g_ref[...] * _dot(a_ref[...], w_ref[...])
    if nk == 1:
        o_ref[...] = x_ref[...] + part
        return

    @pl.when(pl.program_id(2) == 0)
    def _():
        o_ref[...] = x_ref[...]

    o_ref[...] += part


def _project_residual(a, w, x, gate, tm=1024, tn=512, tk=None, name="proj_resid"):
    m, k = a.shape
    n = w.shape[1]
    tk = k if tk is None else tk
    nk = k // tk
    return pl.pallas_call(
        functools.partial(_resid_kernel, nk=nk),
        grid=(m // tm, n // tn, nk),
        in_specs=[
            pl.BlockSpec((tm, tk), lambda i, j, kk: (i, kk)),
            pl.BlockSpec((tk, tn), lambda i, j, kk: (kk, j)),
            pl.BlockSpec((tm, tn), lambda i, j, kk: (i, j)),
            pl.BlockSpec((1, tn), lambda i, j, kk: (0, j)),
        ],
        out_specs=pl.BlockSpec((tm, tn), lambda i, j, kk: (i, j)),
        out_shape=jax.ShapeDtypeStruct((m, n), F32),
        compiler_params=_params("arbitrary", "arbitrary", "arbitrary"),
        name=name,
    )(a, w, x, gate)


def _resid_cat_kernel(a1_ref, a2_ref, w_ref, x_ref, g_ref, o_ref):
    k1 = a1_ref.shape[1]
    part = _dot(a1_ref[...], w_ref[0:k1, :]) + _dot(a2_ref[...], w_ref[k1:, :])
    o_ref[...] = x_ref[...] + g_ref[...] * part


def _project_residual_cat(a1, a2, w, x, gate, tm=1024, tn=512, name="proj_resid_cat"):
    m, k1 = a1.shape
    k2 = a2.shape[1]
    n = w.shape[1]
    return pl.pallas_call(
        _resid_cat_kernel,
        grid=(m // tm, n // tn),
        in_specs=[
            pl.BlockSpec((tm, k1), lambda i, j: (i, 0)),
            pl.BlockSpec((tm, k2), lambda i, j: (i, 0)),
            pl.BlockSpec((k1 + k2, tn), lambda i, j: (0, j)),
            pl.BlockSpec((tm, tn), lambda i, j: (i, j)),
            pl.BlockSpec((1, tn), lambda i, j: (0, j)),
        ],
        out_specs=pl.BlockSpec((tm, tn), lambda i, j: (i, j)),
        out_shape=jax.ShapeDtypeStruct((m, n), F32),
        compiler_params=_params("arbitrary", "arbitrary"),
        name=name,
    )(a1, a2, w, x, gate)


CONV_HALO = 16


def _conv_proj_kernel(a_ref, ap_ref, wb_ref, wc_ref, wu_ref, cw_ref, o_ref, ax_ref):
    i = pl.program_id(0)
    h = CONV_HALO

    @pl.when(pl.program_id(1) == 0)
    def _():
        ax_ref[0:h, :] = ap_ref[...]
        ax_ref[h:, :] = a_ref[...]

    ax = ax_ref[...]
    b = _dot(a_ref[...], wb_ref[...])
    z = _dot(ax, wc_ref[...]) * _dot(ax, wu_ref[...])
    row = lax.broadcasted_iota(jnp.int32, z.shape, 0)
    z = jnp.where(jnp.logical_or(i > 0, row >= h), z, 0.0)
    z1 = pltpu.roll(z, 1, 0)
    z2 = pltpu.roll(z, 2, 0)
    cw = cw_ref[...]
    y = cw[2:3, :] * z[h:, :] + cw[1:2, :] * z1[h:, :] + cw[0:1, :] * z2[h:, :]
    o_ref[...] = (b * y).astype(o_ref.dtype)


def _conv_project(a, w, conv_w, tm=1024, tn=256):
    m, k = a.shape
    n = D_MODEL
    nb = n // tn
    hb = tm // CONV_HALO
    return pl.pallas_call(
        _conv_proj_kernel,
        grid=(m // tm, nb),
        in_specs=[
            pl.BlockSpec((tm, k), lambda i, j: (i, 0), pipeline_mode=pl.Buffered(1)),
            pl.BlockSpec((CONV_HALO, k), lambda i, j: (jnp.maximum(i * hb - 1, 0), 0)),
            pl.BlockSpec((k, tn), lambda i, j: (0, j)),
            pl.BlockSpec((k, tn), lambda i, j: (0, nb + j)),
            pl.BlockSpec((k, tn), lambda i, j: (0, 2 * nb + j)),
            pl.BlockSpec((3, tn), lambda i, j: (0, j)),
        ],
        out_specs=pl.BlockSpec((tm, tn), lambda i, j: (i, j)),
        out_shape=jax.ShapeDtypeStruct((m, n), BF16),
        scratch_shapes=[pltpu.VMEM((tm + CONV_HALO, k), BF16)],
        compiler_params=_params("arbitrary", "arbitrary"),
        name="conv_proj",
    )(a, a, w, w, w, conv_w)


RET_T = 512
RET_HP = 2


def _ret_kernel(lg_ref, q_ref, k_ref, v_ref, g_ref, o_ref, state_ref):
    @pl.when(pl.program_id(1) == 0)
    def _():
        state_ref[...] = jnp.zeros_like(state_ref)

    c = RET_CHUNK
    dh = RET_DIM
    inv_scale = dh ** -0.5
    ri = lax.broadcasted_iota(jnp.int32, (c, c), 0)
    ci = lax.broadcasted_iota(jnp.int32, (c, c), 1)
    rel = (ri - ci).astype(F32)
    pos = lax.broadcasted_iota(jnp.int32, (c, 1), 0).astype(F32)
    decays = []
    for hh in range(RET_HP):
        lg = lg_ref[pl.program_id(0) * RET_HP + hh]
        intra = jnp.where(rel >= 0, jnp.exp(lg * jnp.maximum(rel, 0.0)), 0.0) * inv_scale
        q_decay = jnp.exp(lg * (pos + 1.0))
        k_decay = jnp.exp(lg * (c - 1.0 - pos)) * inv_scale
        chunk_decay = jnp.exp(jnp.full((1, 1), lg * c, F32))
        decays.append((intra, q_decay, k_decay, chunk_decay))

    for t in range(RET_T // c):
        rows = slice(t * c, (t + 1) * c)
        for hh in range(RET_HP):
            intra, q_decay, k_decay, chunk_decay = decays[hh]
            cols = slice(hh * dh, (hh + 1) * dh)
            q = q_ref[rows, cols]
            k = k_ref[rows, cols]
            v = v_ref[rows, cols]
            state = state_ref[hh]
            scores = lax.dot_general(q, k, (((1,), (1,)), ((), ())), preferred_element_type=F32) * intra
            o = _dot(scores.astype(BF16), v)
            o = o + _dot(q, state.astype(BF16)) * q_decay
            kd = (k.astype(F32) * k_decay).astype(BF16)
            kv = lax.dot_general(kd, v, (((0,), (0,)), ((), ())), preferred_element_type=F32)
            state_ref[hh] = state * chunk_decay + kv
            mu = jnp.mean(o, axis=-1, keepdims=True)
            oc = o - mu
            var = jnp.mean(oc * oc, axis=-1, keepdims=True)
            on = oc * lax.rsqrt(var + EPS)
            o_ref[rows, cols] = (on * _silu(g_ref[rows, cols].astype(F32))).astype(o_ref.dtype)


def _retention(p_ret, log_g):
    m = p_ret.shape[0]
    w = RET_HP * RET_DIM
    groups = RET_HEADS // RET_HP

    def spec(part):
        return pl.BlockSpec((RET_T, w), lambda g, s: (s, part * groups + g))

    return pl.pallas_call(
        _ret_kernel,
        grid=(groups, m // RET_T),
        in_specs=[pl.BlockSpec(memory_space=pltpu.SMEM), spec(0), spec(1), spec(2), spec(3)],
        out_specs=pl.BlockSpec((RET_T, w), lambda g, s: (s, g)),
        out_shape=jax.ShapeDtypeStruct((m, RET_W), BF16),
        scratch_shapes=[pltpu.VMEM((RET_HP, RET_DIM, RET_DIM), F32)],
        compiler_params=_params("arbitrary", "arbitrary"),
        name="retention",
    )(log_g, p_ret, p_ret, p_ret, p_ret)


MOBA_KC = 1024
MOBA_FILL = 2048
MOBA_HP = 2
MOBA_BLOCK_SHIFT = MOBA_BLOCK.bit_length() - 1


def _moba_kernel(sl_ref, q_ref, k_ref, v_ref, *refs, n_cast):
    for src_ref, dst_ref in zip(refs[:n_cast], refs[n_cast + 1:2 * n_cast + 1]):
        dst_ref[...] = src_ref[...].astype(dst_ref.dtype)
    o_ref = refs[n_cast]
    ka_ref, va_ref, km_ref, s_ref, acc_ref = refs[2 * n_cast + 1:]
    i = pl.program_id(1)
    bs = MOBA_BLOCK
    d = MOBA_DIM
    kc = MOBA_KC
    hp = MOBA_HP
    seq = k_ref.shape[0]
    heads = range(hp)

    @pl.when(i == 0)
    def _():
        km_ref[...] = jnp.zeros_like(km_ref)
        fr = MOBA_FILL
        lane = lax.broadcasted_iota(jnp.int32, (fr, d), 1)
        sub = lax.broadcasted_iota(jnp.int32, (d, kc), 0)
        key = lax.broadcasted_iota(jnp.int32, (d, kc), 1)

        def fill(c, carry):
            r = pl.multiple_of(c * fr, fr)
            ones_col = jnp.where(lane == 0, 1.0, 0.0).astype(BF16)
            for hh in heads:
                kb = k_ref[pl.ds(r, fr), hh * d:(hh + 1) * d]
                for t in range(fr // kc):
                    chunk = c * (fr // kc) + t
                    onehot = sub == jnp.right_shift(key + chunk * kc, MOBA_BLOCK_SHIFT)
                    ka_ref[hh, chunk, 0:d, :] = kb[t * kc:(t + 1) * kc, :].astype(F32).T.astype(BF16)
                    ka_ref[hh, chunk, d:2 * d, :] = jnp.where(onehot, 1.0, 0.0).astype(BF16)
                va_ref[hh, pl.ds(r, fr), 0:d] = v_ref[pl.ds(r, fr), hh * d:(hh + 1) * d]
                va_ref[hh, pl.ds(r, fr), d:2 * d] = ones_col
                kmean = jnp.mean(kb.astype(F32).reshape(fr // bs, bs, d), axis=1)
                km_ref[hh, pl.ds(pl.multiple_of(c * (fr // bs), fr // bs), fr // bs), :] = kmean
            return carry

        lax.fori_loop(0, seq // fr, fill, 0)

    scale = d ** -0.5
    slopes = [sl_ref[pl.program_id(0) * hp + hh] for hh in heads]

    nbp = -(-(seq // bs) // 8) * 8
    blk = lax.broadcasted_iota(jnp.int32, (nbp, bs), 0)
    blkf = blk.astype(F32)
    q_augs = []
    for hh in heads:
        q = q_ref[:, hh * d:(hh + 1) * d]
        gate = lax.dot_general(km_ref[hh], q, (((1,), (1,)), ((), ())),
                               precision=lax.Precision.HIGHEST, preferred_element_type=F32)[0:nbp, :]
        g = jnp.where(blk < i, gate, -jnp.inf)
        chosen = blk == i
        for _ in range(MOBA_TOPK):
            mx = jnp.max(g, axis=0, keepdims=True)
            idx = jnp.min(jnp.where(g == mx, blkf, 2.0 * d), axis=0, keepdims=True)
            hit = blkf == idx
            chosen = jnp.logical_or(chosen, jnp.logical_and(hit, mx > -jnp.inf))
            g = jnp.where(hit, -jnp.inf, g)
        bias_t = jnp.concatenate(
            [jnp.where(chosen, 0.0, NEG_BIG), jnp.zeros((d - nbp, bs), F32)], axis=0)
        q_augs.append(jnp.concatenate([q.astype(BF16), bias_t.T.astype(BF16)], axis=1))

    def scores(j, causal, hh):
        r = pl.multiple_of(j * kc, kc)
        s = _dot(q_augs[hh], ka_ref[hh, j]) * scale
        rel = lax.broadcasted_iota(jnp.int32, (1, kc), 1) + (r - i * bs)
        s = s + slopes[hh] * rel.astype(F32)
        if causal:
            row = lax.broadcasted_iota(jnp.int32, (bs, kc), 0)
            s = jnp.where(rel <= row, s, NEG_BIG)
        s_ref[hh, j] = s
        part = s[:, 0:d]
        for t in range(1, kc // d):
            part = jnp.maximum(part, s[:, t * d:(t + 1) * d])
        return part

    def sweep(j, mxs, causal):
        return tuple(jnp.maximum(mxs[hh], scores(j, causal, hh)) for hh in heads)

    n_past = i // (kc // bs)
    mxs = tuple(jnp.full((bs, d), -jnp.inf, F32) for _ in heads)
    mxs = lax.fori_loop(0, n_past, lambda j, c: sweep(j, c, False), mxs)
    mxs = sweep(n_past, mxs, True)
    ms = [jnp.max(mxs[hh], axis=1, keepdims=True) for hh in heads]

    acc_ref[...] = jnp.zeros_like(acc_ref)

    def weighted(j, carry):
        r = pl.multiple_of(j * kc, kc)
        for hh in heads:
            p = jnp.exp(s_ref[hh, j] - ms[hh]).astype(BF16)
            acc_ref[hh] += _dot(p, va_ref[hh, pl.ds(r, kc), :])
        return carry

    lax.fori_loop(0, n_past + 1, weighted, 0)
    for hh in heads:
        acc = acc_ref[hh]
        o_ref[:, hh * d:(hh + 1) * d] = (acc[:, 0:d] / acc[:, d:d + 1]).astype(o_ref.dtype)


MOBA_CAST_COLS = 16


def _moba(q_all, kv_all, slopes, cast_ws=()):
    m = q_all.shape[0]
    hp = MOBA_HP
    w = hp * MOBA_DIM
    groups = q_all.shape[1] // w
    nq = m // MOBA_BLOCK
    row_blocks = groups * nq // MOBA_CAST_COLS

    def cast_block(g, i):
        step = g * nq + i
        return (step // MOBA_CAST_COLS, step % MOBA_CAST_COLS)

    cast_specs = [pl.BlockSpec((cw.shape[0] // row_blocks, cw.shape[1] // MOBA_CAST_COLS), cast_block)
                  for cw in cast_ws]

    def kv_spec(part):
        return pl.BlockSpec((m, w), lambda g, i: (0, part * groups + g), pipeline_mode=pl.Buffered(1))

    staged_keys = pltpu.VMEM((hp, m // MOBA_KC, 2 * MOBA_DIM, MOBA_KC), BF16)
    staged_values = pltpu.VMEM((hp, m, 2 * MOBA_DIM), BF16)
    outs = pl.pallas_call(
        functools.partial(_moba_kernel, n_cast=len(cast_ws)),
        grid=(groups, nq),
        in_specs=[
            pl.BlockSpec(memory_space=pltpu.SMEM),
            pl.BlockSpec((MOBA_BLOCK, w), lambda g, i: (i, g)),
            kv_spec(0),
            kv_spec(1),
        ] + cast_specs,
        out_specs=[pl.BlockSpec((MOBA_BLOCK, w), lambda g, i: (i, g))] + cast_specs,
        out_shape=[jax.ShapeDtypeStruct(q_all.shape, BF16)]
        + [jax.ShapeDtypeStruct(cw.shape, BF16) for cw in cast_ws],
        scratch_shapes=[
            staged_keys,
            staged_values,
            pltpu.VMEM((hp, MOBA_DIM, MOBA_DIM), F32),
            pltpu.VMEM((hp, m // MOBA_KC, MOBA_BLOCK, MOBA_KC), F32),
            pltpu.VMEM((hp, MOBA_BLOCK, 2 * MOBA_DIM), F32),
        ],
        compiler_params=_params("arbitrary", "arbitrary"),
        name="moba",
    )(slopes, q_all, kv_all, kv_all, *cast_ws)
    return outs[0], outs[1:]


def _split_mod(mod):
    d = D_MODEL
    return mod[:, :d], mod[:, d:2 * d], mod[:, 2 * d:]


def _ffn(x, c_col, norm_g, mod_w, mod_b, w1, w3, w2_bf16):
    shift, scale, gate = _split_mod(_mod_vector(c_col, mod_w, mod_b))
    h = _norm_modulate(x, norm_g, scale, shift)
    u = _swiglu_up(h, w1, w3)
    return _project_residual(u, w2_bf16, x, gate, tm=512, name="ffn_down")


def kernel(x, c, l0_mix_norm_g, l0_mix_mod_w, l0_mix_mod_b, l0_mix_w_in, l0_mix_w_out, l0_ffn_norm_g, l0_ffn_mod_w, l0_ffn_mod_b, l0_ffn_w1, l0_ffn_w3, l0_ffn_w2, l1_mix_norm_g, l1_mix_mod_w, l1_mix_mod_b, l1_mix_w_in, l1_mix_conv_w, l1_mix_w_out, l1_ffn_norm_g, l1_ffn_mod_w, l1_ffn_mod_b, l1_ffn_w1, l1_ffn_w3, l1_ffn_w2, final_norm_g):
    b, seq, d = x.shape
    xs = x.reshape(b * seq, d)
    c_col = c.reshape(d, 1)

    log_g = jnp.log1p(-jnp.exp2(-5.0 - jnp.arange(RET_HEADS, dtype=F32)))
    slopes = jnp.exp2(-8.0 * jnp.arange(1, MOBA_HEADS + 1, dtype=F32) / MOBA_HEADS)

    shift, scale, gate = _split_mod(_mod_vector(c_col, l0_mix_mod_w, l0_mix_mod_b))
    h = _norm_modulate(xs, l0_mix_norm_g, scale, shift)
    w_in = l0_mix_w_in
    p_ret = _project(h, w_in, 0, 4 * RET_W, BF16, name="mix0_in_ret")
    moba_q = _project(h, w_in, 4 * RET_W, MOBA_W, F32, name="mix0_in_q")
    moba_kv = _project(h, w_in, 4 * RET_W + MOBA_W, 2 * MOBA_W, BF16, name="mix0_in_kv")
    ret = _retention(p_ret, log_g)
    att, (l0_w2, l1_w2) = _moba(moba_q, moba_kv, slopes, (l0_ffn_w2, l1_ffn_w2))
    xs = _project_residual_cat(ret, att, l0_mix_w_out, xs, gate, name="mix0_out")
    xs = _ffn(xs, c_col, l0_ffn_norm_g, l0_ffn_mod_w, l0_ffn_mod_b, l0_ffn_w1, l0_ffn_w3, l0_w2)

    shift, scale, gate = _split_mod(_mod_vector(c_col, l1_mix_mod_w, l1_mix_mod_b))
    h = _norm_modulate(xs, l1_mix_norm_g, scale, shift)
    s = _conv_project(h, l1_mix_w_in, l1_mix_conv_w.reshape(3, d))
    xs = _project_residual(s, l1_mix_w_out, xs, gate, name="mix1_out")
    xs = _ffn(xs, c_col, l1_ffn_norm_g, l1_ffn_mod_w, l1_ffn_mod_b, l1_ffn_w1, l1_ffn_w3, l1_w2)

    return _final_norm(xs, final_norm_g).reshape(b, seq, d)
```

```python
import functools

import jax
import jax.numpy as jnp
from jax import lax
from jax.experimental import pallas as pl
from jax.experimental.pallas import tpu as pltpu

D_MODEL = 4096
SEQ = 8192
RET_HEADS = 8
RET_DIM = 256
RET_CHUNK = 128
MOBA_HEADS = 16
MOBA_DIM = 128
MOBA_BLOCK = 256
MOBA_TOPK = 3
FFN_HIDDEN = 11008
EPS = 1e-6
RET_W = RET_HEADS * RET_DIM
MOBA_W = MOBA_HEADS * MOBA_DIM
N_BLOCKS = SEQ // MOBA_BLOCK

VMEM_LIMIT_BYTES = 56 * 1024 * 1024
NEG_BIG = -1e30

BF16 = jnp.bfloat16
F32 = jnp.float32
LANES = 128


def _params(*sem):
    return pltpu.CompilerParams(dimension_semantics=sem, vmem_limit_bytes=VMEM_LIMIT_BYTES)


def _silu(v):
    return v * jax.nn.sigmoid(v)


MOD_TN = 512
MOD_KC = 128


def _mod_kernel(c_ref, w_ref, b_ref, o_ref, ca_ref):
    lanes = ca_ref.shape[1]

    @pl.when(pl.program_id(0) == 0)
    def _():
        ca_ref[...] = jnp.broadcast_to(_silu(c_ref[...]), ca_ref.shape)

    def body(k, acc):
        r = pl.multiple_of(k * MOD_KC, MOD_KC)
        ca = ca_ref[pl.ds(r, MOD_KC), :]
        p = w_ref[pl.ds(r, MOD_KC), :] * jnp.concatenate([ca] * (MOD_TN // lanes), axis=1)
        return acc + jnp.sum(p.reshape(MOD_KC // 8, 8, MOD_TN), axis=0)

    acc = lax.fori_loop(0, D_MODEL // MOD_KC, body, jnp.zeros((8, MOD_TN), F32), unroll=4)
    o_ref[...] = jnp.sum(acc, axis=0, keepdims=True) + b_ref[...]


def _mod_vector(c_col, w, b):
    n = w.shape[1]
    return pl.pallas_call(
        _mod_kernel,
        grid=(n // MOD_TN,),
        in_specs=[
            pl.BlockSpec((D_MODEL, 1), lambda j: (0, 0)),
            pl.BlockSpec((D_MODEL, MOD_TN), lambda j: (0, j)),
            pl.BlockSpec((1, MOD_TN), lambda j: (0, j)),
        ],
        out_specs=[pl.BlockSpec((1, MOD_TN), lambda j: (0, j)),
                   pl.BlockSpec((D_MODEL, LANES), lambda j: (0, 0))],
        out_shape=[jax.ShapeDtypeStruct((1, n), F32), jax.ShapeDtypeStruct((D_MODEL, LANES), F32)],
        compiler_params=_params("arbitrary"),
        name="adaln_mod",
    )(c_col, w, b.reshape(1, n))


MOD_ACC_COLS = 2048


def _mod_ride(step, last_step, ca_ref, wm_ref, b_ref, acc_ref, mod_ref):
    rows, n = wm_ref.shape

    @pl.when(step == 0)
    def _():
        acc_ref[...] = jnp.zeros_like(acc_ref)

    for c0 in range(0, n, MOD_ACC_COLS):
        cols = slice(c0, c0 + MOD_ACC_COLS)
        part = acc_ref[:, cols]
        for r in range(0, rows, 8):
            ca = jnp.concatenate([ca_ref[r:r + 8, :]] * (MOD_ACC_COLS // LANES), axis=1)
            part = part + wm_ref[r:r + 8, cols] * ca
        acc_ref[:, cols] = part

    @pl.when(step == last_step)
    def _():
        mod_ref[...] = jnp.sum(acc_ref[...], axis=0, keepdims=True) + b_ref[...]


NORM_TM = 256


NORM_RC = 16


def _norm_rows(x_ref, o_ref, mult, add):
    def body(r, carry):
        rows = pl.ds(pl.multiple_of(r * NORM_RC, NORM_RC), NORM_RC)
        x = x_ref[rows, :]
        ms = jnp.mean(x * x, axis=-1, keepdims=True)
        y = x * lax.rsqrt(ms + EPS) * mult
        if add is not None:
            y = y + add
        o_ref[rows, :] = y.astype(o_ref.dtype)
        return carry

    lax.fori_loop(0, x_ref.shape[0] // NORM_RC, body, 0, unroll=8)


def _normmod_kernel(x_ref, g_ref, sc_ref, sh_ref, o_ref):
    _norm_rows(x_ref, o_ref, g_ref[...] * (1.0 + sc_ref[...]), sh_ref[...])


def _norm_kernel(x_ref, g_ref, o_ref):
    _norm_rows(x_ref, o_ref, g_ref[...], None)


def _norm_modulate(x, g, scale, shift):
    m, d = x.shape
    row = pl.BlockSpec((NORM_TM, d), lambda i: (i, 0))
    vec = pl.BlockSpec((1, d), lambda i: (0, 0))
    return pl.pallas_call(
        _normmod_kernel,
        grid=(m // NORM_TM,),
        in_specs=[row, vec, vec, vec],
        out_specs=row,
        out_shape=jax.ShapeDtypeStruct((m, d), BF16),
        compiler_params=_params("arbitrary"),
        name="norm_modulate",
    )(x, g.reshape(1, d), scale, shift)


def _final_norm(x, g):
    m, d = x.shape
    row = pl.BlockSpec((NORM_TM, d), lambda i: (i, 0))
    vec = pl.BlockSpec((1, d), lambda i: (0, 0))
    return pl.pallas_call(
        _norm_kernel,
        grid=(m // NORM_TM,),
        in_specs=[row, vec],
        out_specs=row,
        out_shape=jax.ShapeDtypeStruct((m, d), F32),
        compiler_params=_params("arbitrary"),
        name="final_norm",
    )(x, g.reshape(1, d))


def _dot(a, b):
    return jnp.dot(a, b.astype(BF16), preferred_element_type=F32)


def _proj_kernel(a_ref, w_ref, o_ref):
    o_ref[...] = _dot(a_ref[...], w_ref[...]).astype(o_ref.dtype)


def _project(a, w, col0, ncols, out_dtype, tm=1024, tn=512, name="proj"):
    m, k = a.shape
    cb = col0 // tn
    return pl.pallas_call(
        _proj_kernel,
        grid=(m // tm, ncols // tn),
        in_specs=[
            pl.BlockSpec((tm, k), lambda i, j: (i, 0)),
            pl.BlockSpec((k, tn), lambda i, j: (0, cb + j)),
        ],
        out_specs=pl.BlockSpec((tm, tn), lambda i, j: (i, j)),
        out_shape=jax.ShapeDtypeStruct((m, ncols), out_dtype),
        compiler_params=_params("arbitrary", "arbitrary"),
        name=name,
    )(a, w)


def _swiglu_kernel(a_ref, w1_ref, w3_ref, o_ref):
    a = a_ref[...]
    h1 = _dot(a, w1_ref[...])
    h3 = _dot(a, w3_ref[...])
    o_ref[...] = (_silu(h1) * h3).astype(o_ref.dtype)


def _swiglu_up(a, w1, w3, tm=2048, tn=256):
    m, k = a.shape
    n = w1.shape[1]
    wspec = pl.BlockSpec((k, tn), lambda i, j: (0, j))
    return pl.pallas_call(
        _swiglu_kernel,
        grid=(m // tm, n // tn),
        in_specs=[pl.BlockSpec((tm, k), lambda i, j: (i, 0)), wspec, wspec],
        out_specs=pl.BlockSpec((tm, tn), lambda i, j: (i, j)),
        out_shape=jax.ShapeDtypeStruct((m, n), BF16),
        compiler_params=_params("arbitrary", "arbitrary"),
        name="swiglu_up",
    )(a, w1, w3)


def _resid_kernel(a_ref, w_ref, x_ref, g_ref, o_ref, *, nk):
    part = g_ref[...] * _dot(a_ref[...], w_ref[...])
    if nk == 1:
        o_ref[...] = x_ref[...] + part
        return

    @pl.when(pl.program_id(2) == 0)
    def _():
        o_ref[...] = x_ref[...]

    o_ref[...] += part


def _project_residual(a, w, x, gate, tm=1024, tn=512, tk=None, name="proj_resid"):
    m, k = a.shape
    n = w.shape[1]
    tk = k if tk is None else tk
    nk = k // tk
    return pl.pallas_call(
        functools.partial(_resid_kernel, nk=nk),
        grid=(m // tm, n // tn, nk),
        in_specs=[
            pl.BlockSpec((tm, tk), lambda i, j, kk: (i, kk)),
            pl.BlockSpec((tk, tn), lambda i, j, kk: (kk, j)),
            pl.BlockSpec((tm, tn), lambda i, j, kk: (i, j)),
            pl.BlockSpec((1, tn), lambda i, j, kk: (0, j)),
        ],
        out_specs=pl.BlockSpec((tm, tn), lambda i, j, kk: (i, j)),
        out_shape=jax.ShapeDtypeStruct((m, n), F32),
        compiler_params=_params("arbitrary", "arbitrary", "arbitrary"),
        name=name,
    )(a, w, x, gate)


def _resid_cat_kernel(a1_ref, a2_ref, w_ref, x_ref, g_ref, o_ref):
    k1 = a1_ref.shape[1]
    part = _dot(a1_ref[...], w_ref[0:k1, :]) + _dot(a2_ref[...], w_ref[k1:, :])
    o_ref[...] = x_ref[...] + g_ref[...] * part


def _project_residual_cat(a1, a2, w, x, gate, tm=1024, tn=512, name="proj_resid_cat"):
    m, k1 = a1.shape
    k2 = a2.shape[1]
    n = w.shape[1]
    return pl.pallas_call(
        _resid_cat_kernel,
        grid=(m // tm, n // tn),
        in_specs=[
            pl.BlockSpec((tm, k1), lambda i, j: (i, 0)),
            pl.BlockSpec((tm, k2), lambda i, j: (i, 0)),
            pl.BlockSpec((k1 + k2, tn), lambda i, j: (0, j)),
            pl.BlockSpec((tm, tn), lambda i, j: (i, j)),
            pl.BlockSpec((1, tn), lambda i, j: (0, j)),
        ],
        out_specs=pl.BlockSpec((tm, tn), lambda i, j: (i, j)),
        out_shape=jax.ShapeDtypeStruct((m, n), F32),
        compiler_params=_params("arbitrary", "arbitrary"),
        name=name,
    )(a1, a2, w, x, gate)


CONV_HALO = 16


def _conv_proj_kernel(a_ref, ap_ref, wb_ref, wc_ref, wu_ref, cw_ref, o_ref, ax_ref):
    i = pl.program_id(0)
    h = CONV_HALO

    @pl.when(pl.program_id(1) == 0)
    def _():
        ax_ref[0:h, :] = ap_ref[...]
        ax_ref[h:, :] = a_ref[...]

    ax = ax_ref[...]
    b = _dot(a_ref[...], wb_ref[...])
    z = _dot(ax, wc_ref[...]) * _dot(ax, wu_ref[...])
    row = lax.broadcasted_iota(jnp.int32, z.shape, 0)
    z = jnp.where(jnp.logical_or(i > 0, row >= h), z, 0.0)
    z1 = pltpu.roll(z, 1, 0)
    z2 = pltpu.roll(z, 2, 0)
    cw = cw_ref[...]
    y = cw[2:3, :] * z[h:, :] + cw[1:2, :] * z1[h:, :] + cw[0:1, :] * z2[h:, :]
    o_ref[...] = (b * y).astype(o_ref.dtype)


def _conv_project(a, w, conv_w, tm=1024, tn=256):
    m, k = a.shape
    n = D_MODEL
    nb = n // tn
    hb = tm // CONV_HALO
    return pl.pallas_call(
        _conv_proj_kernel,
        grid=(m // tm, nb),
        in_specs=[
            pl.BlockSpec((tm, k), lambda i, j: (i, 0), pipeline_mode=pl.Buffered(1)),
            pl.BlockSpec((CONV_HALO, k), lambda i, j: (jnp.maximum(i * hb - 1, 0), 0)),
            pl.BlockSpec((k, tn), lambda i, j: (0, j)),
            pl.BlockSpec((k, tn), lambda i, j: (0, nb + j)),
            pl.BlockSpec((k, tn), lambda i, j: (0, 2 * nb + j)),
            pl.BlockSpec((3, tn), lambda i, j: (0, j)),
        ],
        out_specs=pl.BlockSpec((tm, tn), lambda i, j: (i, j)),
        out_shape=jax.ShapeDtypeStruct((m, n), BF16),
        scratch_shapes=[pltpu.VMEM((tm + CONV_HALO, k), BF16)],
        compiler_params=_params("arbitrary", "arbitrary"),
        name="conv_proj",
    )(a, a, w, w, w, conv_w)


RET_T = 512
RET_HP = 2


def _ret_kernel(lg_ref, q_ref, k_ref, v_ref, g_ref, o_ref, state_ref):
    @pl.when(pl.program_id(1) == 0)
    def _():
        state_ref[...] = jnp.zeros_like(state_ref)

    c = RET_CHUNK
    dh = RET_DIM
    inv_scale = dh ** -0.5
    ri = lax.broadcasted_iota(jnp.int32, (c, c), 0)
    ci = lax.broadcasted_iota(jnp.int32, (c, c), 1)
    rel = (ri - ci).astype(F32)
    pos = lax.broadcasted_iota(jnp.int32, (c, 1), 0).astype(F32)
    decays = []
    for hh in range(RET_HP):
        lg = lg_ref[pl.program_id(0) * RET_HP + hh]
        intra = jnp.where(rel >= 0, jnp.exp(lg * jnp.maximum(rel, 0.0)), 0.0) * inv_scale
        q_decay = jnp.exp(lg * (pos + 1.0))
        k_decay = jnp.exp(lg * (c - 1.0 - pos)) * inv_scale
        chunk_decay = jnp.exp(jnp.full((1, 1), lg * c, F32))
        decays.append((intra, q_decay, k_decay, chunk_decay))

    for t in range(RET_T // c):
        rows = slice(t * c, (t + 1) * c)
        for hh in range(RET_HP):
            intra, q_decay, k_decay, chunk_decay = decays[hh]
            cols = slice(hh * dh, (hh + 1) * dh)
            q = q_ref[rows, cols]
            k = k_ref[rows, cols]
            v = v_ref[rows, cols]
            state = state_ref[hh]
            scores = lax.dot_general(q, k, (((1,), (1,)), ((), ())), preferred_element_type=F32) * intra
            o = _dot(scores.astype(BF16), v)
            o = o + _dot(q, state.astype(BF16)) * q_decay
            kd = (k.astype(F32) * k_decay).astype(BF16)
            kv = lax.dot_general(kd, v, (((0,), (0,)), ((), ())), preferred_element_type=F32)
            state_ref[hh] = state * chunk_decay + kv
            mu = jnp.mean(o, axis=-1, keepdims=True)
            oc = o - mu
            var = jnp.mean(oc * oc, axis=-1, keepdims=True)
            on = oc * lax.rsqrt(var + EPS)
            o_ref[rows, cols] = (on * _silu(g_ref[rows, cols].astype(F32))).astype(o_ref.dtype)


def _retention(p_ret, log_g):
    m = p_ret.shape[0]
    w = RET_HP * RET_DIM
    groups = RET_HEADS // RET_HP

    def spec(part):
        return pl.BlockSpec((RET_T, w), lambda g, s: (s, part * groups + g))

    return pl.pallas_call(
        _ret_kernel,
        grid=(groups, m // RET_T),
        in_specs=[pl.BlockSpec(memory_space=pltpu.SMEM), spec(0), spec(1), spec(2), spec(3)],
        out_specs=pl.BlockSpec((RET_T, w), lambda g, s: (s, g)),
        out_shape=jax.ShapeDtypeStruct((m, RET_W), BF16),
        scratch_shapes=[pltpu.VMEM((RET_HP, RET_DIM, RET_DIM), F32)],
        compiler_params=_params("arbitrary", "arbitrary"),
        name="retention",
    )(log_g, p_ret, p_ret, p_ret, p_ret)


MOBA_KC = 1024
MOBA_FILL = 2048
MOBA_HP = 2
MOBA_BLOCK_SHIFT = MOBA_BLOCK.bit_length() - 1


def _moba_kernel(sl_ref, q_ref, k_ref, v_ref, *refs, n_cast, n_mods):
    n_in = n_cast + (1 + 2 * n_mods if n_mods else 0)
    o_ref = refs[n_in]
    ka_ref, va_ref, km_ref, s_ref, acc_ref = refs[n_in + 1 + n_cast + 2 * n_mods:]
    for t in range(n_cast):
        refs[n_in + 1 + t][...] = refs[t][...].astype(refs[n_in + 1 + t].dtype)
    step = pl.program_id(0) * pl.num_programs(1) + pl.program_id(1)
    last_step = pl.num_programs(0) * pl.num_programs(1) - 1
    for t in range(n_mods):
        wm_ref, b_ref = refs[n_cast + 1 + 2 * t], refs[n_cast + 2 + 2 * t]
        macc_ref, mod_ref = refs[n_in + 1 + n_cast + 2 * t], refs[n_in + 2 + n_cast + 2 * t]
        _mod_ride(step, last_step, refs[n_cast], wm_ref, b_ref, macc_ref, mod_ref)
    i = pl.program_id(1)
    bs = MOBA_BLOCK
    d = MOBA_DIM
    kc = MOBA_KC
    hp = MOBA_HP
    seq = k_ref.shape[0]
    heads = range(hp)

    @pl.when(i == 0)
    def _():
        km_ref[...] = jnp.zeros_like(km_ref)
        fr = MOBA_FILL
        lane = lax.broadcasted_iota(jnp.int32, (fr, d), 1)
        sub = lax.broadcasted_iota(jnp.int32, (d, kc), 0)
        key = lax.broadcasted_iota(jnp.int32, (d, kc), 1)

        def fill(c, carry):
            r = pl.multiple_of(c * fr, fr)
            ones_col = jnp.where(lane == 0, 1.0, 0.0).astype(BF16)
            for hh in heads:
                kb = k_ref[pl.ds(r, fr), hh * d:(hh + 1) * d]
                for t in range(fr // kc):
                    chunk = c * (fr // kc) + t
                    onehot = sub == jnp.right_shift(key + chunk * kc, MOBA_BLOCK_SHIFT)
                    ka_ref[hh, chunk, 0:d, :] = kb[t * kc:(t + 1) * kc, :].astype(F32).T.astype(BF16)
                    ka_ref[hh, chunk, d:2 * d, :] = jnp.where(onehot, 1.0, 0.0).astype(BF16)
                va_ref[hh, pl.ds(r, fr), 0:d] = v_ref[pl.ds(r, fr), hh * d:(hh + 1) * d]
                va_ref[hh, pl.ds(r, fr), d:2 * d] = ones_col
                kmean = jnp.mean(kb.astype(F32).reshape(fr // bs, bs, d), axis=1)
                km_ref[hh, pl.ds(pl.multiple_of(c * (fr // bs), fr // bs), fr // bs), :] = kmean
            return carry

        lax.fori_loop(0, seq // fr, fill, 0)

    scale = d ** -0.5
    slopes = [sl_ref[pl.program_id(0) * hp + hh] for hh in heads]

    nbp = -(-(seq // bs) // 8) * 8
    blk = lax.broadcasted_iota(jnp.int32, (nbp, bs), 0)
    blkf = blk.astype(F32)
    q_augs = []
    for hh in heads:
        q = q_ref[:, hh * d:(hh + 1) * d]
        gate = lax.dot_general(km_ref[hh], q, (((1,), (1,)), ((), ())),
                               precision=lax.Precision.HIGHEST, preferred_element_type=F32)[0:nbp, :]
        g = jnp.where(blk < i, gate, -jnp.inf)
        chosen = blk == i
        for _ in range(MOBA_TOPK):
            mx = jnp.max(g, axis=0, keepdims=True)
            idx = jnp.min(jnp.where(g == mx, blkf, 2.0 * d), axis=0, keepdims=True)
            hit = blkf == idx
            chosen = jnp.logical_or(chosen, jnp.logical_and(hit, mx > -jnp.inf))
            g = jnp.where(hit, -jnp.inf, g)
        bias_t = jnp.concatenate(
            [jnp.where(chosen, 0.0, NEG_BIG), jnp.zeros((d - nbp, bs), F32)], axis=0)
        q_augs.append(jnp.concatenate([q.astype(BF16), bias_t.T.astype(BF16)], axis=1))

    def scores(j, causal, hh):
        r = pl.multiple_of(j * kc, kc)
        s = _dot(q_augs[hh], ka_ref[hh, j]) * scale
        rel = lax.broadcasted_iota(jnp.int32, (1, kc), 1) + (r - i * bs)
        s = s + slopes[hh] * rel.astype(F32)
        if causal:
            row = lax.broadcasted_iota(jnp.int32, (bs, kc), 0)
            s = jnp.where(rel <= row, s, NEG_BIG)
        s_ref[hh, j] = s
        part = s[:, 0:d]
        for t in range(1, kc // d):
            part = jnp.maximum(part, s[:, t * d:(t + 1) * d])
        return part

    def sweep(j, mxs, causal):
        return tuple(jnp.maximum(mxs[hh], scores(j, causal, hh)) for hh in heads)

    n_past = i // (kc // bs)
    mxs = tuple(jnp.full((bs, d), -jnp.inf, F32) for _ in heads)
    mxs = lax.fori_loop(0, n_past, lambda j, c: sweep(j, c, False), mxs)
    mxs = sweep(n_past, mxs, True)
    ms = [jnp.max(mxs[hh], axis=1, keepdims=True) for hh in heads]

    acc_ref[...] = jnp.zeros_like(acc_ref)

    def weighted(j, carry):
        r = pl.multiple_of(j * kc, kc)
        for hh in heads:
            p = jnp.exp(s_ref[hh, j] - ms[hh]).astype(BF16)
            acc_ref[hh] += _dot(p, va_ref[hh, pl.ds(r, kc), :])
        return carry

    lax.fori_loop(0, n_past + 1, weighted, 0)
    for hh in heads:
        acc = acc_ref[hh]
        o_ref[:, hh * d:(hh + 1) * d] = (acc[:, 0:d] / acc[:, d:d + 1]).astype(o_ref.dtype)


MOBA_CAST_COLS = 16


def _moba(q_all, kv_all, slopes, cast_ws=(), ca_rep=None, mods=()):
    m = q_all.shape[0]
    hp = MOBA_HP
    w = hp * MOBA_DIM
    groups = q_all.shape[1] // w
    nq = m // MOBA_BLOCK
    row_blocks = groups * nq // MOBA_CAST_COLS

    def cast_block(g, i):
        step = g * nq + i
        return (step // MOBA_CAST_COLS, step % MOBA_CAST_COLS)

    cast_specs = [pl.BlockSpec((cw.shape[0] // row_blocks, cw.shape[1] // MOBA_CAST_COLS), cast_block)
                  for cw in cast_ws]

    steps = groups * nq
    slab = lambda g, i: (g * nq + i, 0)
    whole = lambda g, i: (0, 0)
    mod_in_specs, mod_out_specs, mod_out_shapes, mod_args = [], [], [], []
    if mods:
        mod_in_specs.append(pl.BlockSpec((D_MODEL // steps, LANES), slab))
        mod_args.append(ca_rep)
    for w_mod, b_mod in mods:
        n = w_mod.shape[1]
        mod_in_specs += [pl.BlockSpec((D_MODEL // steps, n), slab), pl.BlockSpec((1, n), whole)]
        mod_out_specs += [pl.BlockSpec((8, n), whole), pl.BlockSpec((1, n), whole)]
        mod_out_shapes += [jax.ShapeDtypeStruct((8, n), F32), jax.ShapeDtypeStruct((1, n), F32)]
        mod_args += [w_mod, b_mod.reshape(1, n)]

    def kv_spec(part):
        return pl.BlockSpec((m, w), lambda g, i: (0, part * groups + g), pipeline_mode=pl.Buffered(1))

    staged_keys = pltpu.VMEM((hp, m // MOBA_KC, 2 * MOBA_DIM, MOBA_KC), BF16)
    staged_values = pltpu.VMEM((hp, m, 2 * MOBA_DIM), BF16)
    outs = pl.pallas_call(
        functools.partial(_moba_kernel, n_cast=len(cast_ws), n_mods=len(mods)),
        grid=(groups, nq),
        in_specs=[
            pl.BlockSpec(memory_space=pltpu.SMEM),
            pl.BlockSpec((MOBA_BLOCK, w), lambda g, i: (i, g)),
            kv_spec(0),
            kv_spec(1),
        ] + cast_specs + mod_in_specs,
        out_specs=[pl.BlockSpec((MOBA_BLOCK, w), lambda g, i: (i, g))] + cast_specs + mod_out_specs,
        out_shape=[jax.ShapeDtypeStruct(q_all.shape, BF16)]
        + [jax.ShapeDtypeStruct(cw.shape, BF16) for cw in cast_ws] + mod_out_shapes,
        scratch_shapes=[
            staged_keys,
            staged_values,
            pltpu.VMEM((hp, MOBA_DIM, MOBA_DIM), F32),
            pltpu.VMEM((hp, m // MOBA_KC, MOBA_BLOCK, MOBA_KC), F32),
            pltpu.VMEM((hp, MOBA_BLOCK, 2 * MOBA_DIM), F32),
        ],
        compiler_params=_params("arbitrary", "arbitrary"),
        name="moba",
    )(slopes, q_all, kv_all, kv_all, *cast_ws, *mod_args)
    nc = len(cast_ws)
    return outs[0], outs[1:1 + nc], [outs[2 + nc + 2 * t] for t in range(len(mods))]


def _split_mod(mod):
    d = D_MODEL
    return mod[:, :d], mod[:, d:2 * d], mod[:, 2 * d:]


def _ffn(x, mod, norm_g, w1, w3, w2_bf16):
    shift, scale, gate = _split_mod(mod)
    h = _norm_modulate(x, norm_g, scale, shift)
    u = _swiglu_up(h, w1, w3)
    return _project_residual(u, w2_bf16, x, gate, tm=512, name="ffn_down")


def kernel(x, c, l0_mix_norm_g, l0_mix_mod_w, l0_mix_mod_b, l0_mix_w_in, l0_mix_w_out, l0_ffn_norm_g, l0_ffn_mod_w, l0_ffn_mod_b, l0_ffn_w1, l0_ffn_w3, l0_ffn_w2, l1_mix_norm_g, l1_mix_mod_w, l1_mix_mod_b, l1_mix_w_in, l1_mix_conv_w, l1_mix_w_out, l1_ffn_norm_g, l1_ffn_mod_w, l1_ffn_mod_b, l1_ffn_w1, l1_ffn_w3, l1_ffn_w2, final_norm_g):
    b, seq, d = x.shape
    xs = x.reshape(b * seq, d)
    c_col = c.reshape(d, 1)

    log_g = jnp.log1p(-jnp.exp2(-5.0 - jnp.arange(RET_HEADS, dtype=F32)))
    slopes = jnp.exp2(-8.0 * jnp.arange(1, MOBA_HEADS + 1, dtype=F32) / MOBA_HEADS)

    mod, ca_rep = _mod_vector(c_col, l0_mix_mod_w, l0_mix_mod_b)
    shift, scale, gate = _split_mod(mod)
    h = _norm_modulate(xs, l0_mix_norm_g, scale, shift)
    w_in = l0_mix_w_in
    p_ret = _project(h, w_in, 0, 4 * RET_W, BF16, name="mix0_in_ret")
    moba_q = _project(h, w_in, 4 * RET_W, MOBA_W, F32, name="mix0_in_q")
    moba_kv = _project(h, w_in, 4 * RET_W + MOBA_W, 2 * MOBA_W, BF16, name="mix0_in_kv")
    ret = _retention(p_ret, log_g)
    att, (l0_w2, l1_w2), (l0_ffn_mod, l1_mix_mod) = _moba(
        moba_q, moba_kv, slopes, (l0_ffn_w2, l1_ffn_w2), ca_rep,
        ((l0_ffn_mod_w, l0_ffn_mod_b), (l1_mix_mod_w, l1_mix_mod_b)))
    xs = _project_residual_cat(ret, att, l0_mix_w_out, xs, gate, name="mix0_out")
    xs = _ffn(xs, l0_ffn_mod, l0_ffn_norm_g, l0_ffn_w1, l0_ffn_w3, l0_w2)

    shift, scale, gate = _split_mod(l1_mix_mod)
    h = _norm_modulate(xs, l1_mix_norm_g, scale, shift)
    s = _conv_project(h, l1_mix_w_in, l1_mix_conv_w.reshape(3, d))
    xs = _project_residual(s, l1_mix_w_out, xs, gate, name="mix1_out")
    l1_ffn_mod, _ = _mod_vector(c_col, l1_ffn_mod_w, l1_ffn_mod_b)
    xs = _ffn(xs, l1_ffn_mod, l1_ffn_norm_g, l1_ffn_w1, l1_ffn_w3, l1_w2)

    return _final_norm(xs, final_norm_g).reshape(b, seq, d)
```

```python
import functools

import jax
import jax.numpy as jnp
from jax import lax
from jax.experimental import pallas as pl
from jax.experimental.pallas import tpu as pltpu

D_MODEL = 4096
SEQ = 8192
RET_HEADS = 8
RET_DIM = 256
RET_CHUNK = 128
MOBA_HEADS = 16
MOBA_DIM = 128
MOBA_BLOCK = 256
MOBA_TOPK = 3
FFN_HIDDEN = 11008
EPS = 1e-6
RET_W = RET_HEADS * RET_DIM
MOBA_W = MOBA_HEADS * MOBA_DIM
N_BLOCKS = SEQ // MOBA_BLOCK

VMEM_LIMIT_BYTES = 56 * 1024 * 1024
NEG_BIG = -1e30

BF16 = jnp.bfloat16
F32 = jnp.float32
LANES = 128


def _params(*sem):
    return pltpu.CompilerParams(dimension_semantics=sem, vmem_limit_bytes=VMEM_LIMIT_BYTES)


def _silu(v):
    return v * jax.nn.sigmoid(v)


MOD_TN = 512
MOD_KC = 128


def _mod_kernel(c_ref, w_ref, b_ref, o_ref, ca_ref):
    lanes = ca_ref.shape[1]

    @pl.when(pl.program_id(0) == 0)
    def _():
        ca_ref[...] = jnp.broadcast_to(_silu(c_ref[...]), ca_ref.shape)

    def body(k, acc):
        r = pl.multiple_of(k * MOD_KC, MOD_KC)
        ca = ca_ref[pl.ds(r, MOD_KC), :]
        p = w_ref[pl.ds(r, MOD_KC), :] * jnp.concatenate([ca] * (MOD_TN // lanes), axis=1)
        return acc + jnp.sum(p.reshape(MOD_KC // 8, 8, MOD_TN), axis=0)

    acc = lax.fori_loop(0, D_MODEL // MOD_KC, body, jnp.zeros((8, MOD_TN), F32), unroll=4)
    o_ref[...] = jnp.sum(acc, axis=0, keepdims=True) + b_ref[...]


def _mod_vector(c_col, w, b):
    n = w.shape[1]
    return pl.pallas_call(
        _mod_kernel,
        grid=(n // MOD_TN,),
        in_specs=[
            pl.BlockSpec((D_MODEL, 1), lambda j: (0, 0)),
            pl.BlockSpec((D_MODEL, MOD_TN), lambda j: (0, j)),
            pl.BlockSpec((1, MOD_TN), lambda j: (0, j)),
        ],
        out_specs=[pl.BlockSpec((1, MOD_TN), lambda j: (0, j)),
                   pl.BlockSpec((D_MODEL, LANES), lambda j: (0, 0))],
        out_shape=[jax.ShapeDtypeStruct((1, n), F32), jax.ShapeDtypeStruct((D_MODEL, LANES), F32)],
        compiler_params=_params("arbitrary"),
        name="adaln_mod",
    )(c_col, w, b.reshape(1, n))


MOD_ACC_COLS = 2048


def _mod_ride(step, last_step, ca_ref, wm_ref, b_ref, acc_ref, mod_ref):
    rows, n = wm_ref.shape

    @pl.when(step == 0)
    def _():
        acc_ref[...] = jnp.zeros_like(acc_ref)

    for c0 in range(0, n, MOD_ACC_COLS):
        cols = slice(c0, c0 + MOD_ACC_COLS)
        part = acc_ref[:, cols]
        for r in range(0, rows, 8):
            ca = jnp.concatenate([ca_ref[r:r + 8, :]] * (MOD_ACC_COLS // LANES), axis=1)
            part = part + wm_ref[r:r + 8, cols] * ca
        acc_ref[:, cols] = part

    @pl.when(step == last_step)
    def _():
        mod_ref[...] = jnp.sum(acc_ref[...], axis=0, keepdims=True) + b_ref[...]


NORM_TM = 256


NORM_RC = 16


def _norm_rows(x_ref, o_ref, mult, add):
    def body(r, carry):
        rows = pl.ds(pl.multiple_of(r * NORM_RC, NORM_RC), NORM_RC)
        x = x_ref[rows, :]
        ms = jnp.mean(x * x, axis=-1, keepdims=True)
        y = x * lax.rsqrt(ms + EPS) * mult
        if add is not None:
            y = y + add
        o_ref[rows, :] = y.astype(o_ref.dtype)
        return carry

    lax.fori_loop(0, x_ref.shape[0] // NORM_RC, body, 0, unroll=8)


def _normmod_kernel(x_ref, g_ref, sc_ref, sh_ref, o_ref):
    _norm_rows(x_ref, o_ref, g_ref[...] * (1.0 + sc_ref[...]), sh_ref[...])


def _norm_kernel(x_ref, g_ref, o_ref):
    _norm_rows(x_ref, o_ref, g_ref[...], None)


def _norm_modulate(x, g, scale, shift):
    m, d = x.shape
    row = pl.BlockSpec((NORM_TM, d), lambda i: (i, 0))
    vec = pl.BlockSpec((1, d), lambda i: (0, 0))
    return pl.pallas_call(
        _normmod_kernel,
        grid=(m // NORM_TM,),
        in_specs=[row, vec, vec, vec],
        out_specs=row,
        out_shape=jax.ShapeDtypeStruct((m, d), BF16),
        compiler_params=_params("arbitrary"),
        name="norm_modulate",
    )(x, g.reshape(1, d), scale, shift)


def _final_norm(x, g):
    m, d = x.shape
    row = pl.BlockSpec((NORM_TM, d), lambda i: (i, 0))
    vec = pl.BlockSpec((1, d), lambda i: (0, 0))
    return pl.pallas_call(
        _norm_kernel,
        grid=(m // NORM_TM,),
        in_specs=[row, vec],
        out_specs=row,
        out_shape=jax.ShapeDtypeStruct((m, d), F32),
        compiler_params=_params("arbitrary"),
        name="final_norm",
    )(x, g.reshape(1, d))


def _dot(a, b):
    return jnp.dot(a, b.astype(BF16), preferred_element_type=F32)


def _proj_kernel(a_ref, w_ref, o_ref):
    o_ref[...] = _dot(a_ref[...], w_ref[...]).astype(o_ref.dtype)


def _project(a, w, col0, ncols, out_dtype, tm=1024, tn=512, name="proj"):
    m, k = a.shape
    cb = col0 // tn
    return pl.pallas_call(
        _proj_kernel,
        grid=(m // tm, ncols // tn),
        in_specs=[
            pl.BlockSpec((tm, k), lambda i, j: (i, 0)),
            pl.BlockSpec((k, tn), lambda i, j: (0, cb + j)),
        ],
        out_specs=pl.BlockSpec((tm, tn), lambda i, j: (i, j)),
        out_shape=jax.ShapeDtypeStruct((m, ncols), out_dtype),
        compiler_params=_params("arbitrary", "arbitrary"),
        name=name,
    )(a, w)


def _swiglu_kernel(a_ref, w1_ref, w3_ref, o_ref):
    a = a_ref[...]
    h1 = _dot(a, w1_ref[...])
    h3 = _dot(a, w3_ref[...])
    o_ref[...] = (_silu(h1) * h3).astype(o_ref.dtype)


def _swiglu_up(a, w1, w3, tm=2048, tn=256):
    m, k = a.shape
    n = w1.shape[1]
    wspec = pl.BlockSpec((k, tn), lambda i, j: (0, j))
    return pl.pallas_call(
        _swiglu_kernel,
        grid=(m // tm, n // tn),
        in_specs=[pl.BlockSpec((tm, k), lambda i, j: (i, 0)), wspec, wspec],
        out_specs=pl.BlockSpec((tm, tn), lambda i, j: (i, j)),
        out_shape=jax.ShapeDtypeStruct((m, n), BF16),
        compiler_params=_params("arbitrary", "arbitrary"),
        name="swiglu_up",
    )(a, w1, w3)


def _resid_kernel(a_ref, w_ref, x_ref, g_ref, o_ref, *, nk):
    part = g_ref[...] * _dot(a_ref[...], w_ref[...])
    if nk == 1:
        o_ref[...] = x_ref[...] + part
        return

    @pl.when(pl.program_id(2) == 0)
    def _():
        o_ref[...] = x_ref[...]

    o_ref[...] += part


def _project_residual(a, w, x, gate, tm=1024, tn=512, tk=None, name="proj_resid"):
    m, k = a.shape
    n = w.shape[1]
    tk = k if tk is None else tk
    nk = k // tk
    return pl.pallas_call(
        functools.partial(_resid_kernel, nk=nk),
        grid=(m // tm, n // tn, nk),
        in_specs=[
            pl.BlockSpec((tm, tk), lambda i, j, kk: (i, kk)),
            pl.BlockSpec((tk, tn), lambda i, j, kk: (kk, j)),
            pl.BlockSpec((tm, tn), lambda i, j, kk: (i, j)),
            pl.BlockSpec((1, tn), lambda i, j, kk: (0, j)),
        ],
        out_specs=pl.BlockSpec((tm, tn), lambda i, j, kk: (i, j)),
        out_shape=jax.ShapeDtypeStruct((m, n), F32),
        compiler_params=_params("arbitrary", "arbitrary", "arbitrary"),
        name=name,
    )(a, w, x, gate)


def _resid_cat_kernel(a1_ref, a2_ref, w_ref, x_ref, g_ref, o_ref):
    k1 = a1_ref.shape[1]
    part = _dot(a1_ref[...], w_ref[0:k1, :]) + _dot(a2_ref[...], w_ref[k1:, :])
    o_ref[...] = x_ref[...] + g_ref[...] * part


def _project_residual_cat(a1, a2, w, x, gate, tm=1024, tn=512, name="proj_resid_cat"):
    m, k1 = a1.shape
    k2 = a2.shape[1]
    n = w.shape[1]
    return pl.pallas_call(
        _resid_cat_kernel,
        grid=(m // tm, n // tn),
        in_specs=[
            pl.BlockSpec((tm, k1), lambda i, j: (i, 0)),
            pl.BlockSpec((tm, k2), lambda i, j: (i, 0)),
            pl.BlockSpec((k1 + k2, tn), lambda i, j: (0, j)),
            pl.BlockSpec((tm, tn), lambda i, j: (i, j)),
            pl.BlockSpec((1, tn), lambda i, j: (0, j)),
        ],
        out_specs=pl.BlockSpec((tm, tn), lambda i, j: (i, j)),
        out_shape=jax.ShapeDtypeStruct((m, n), F32),
        compiler_params=_params("arbitrary", "arbitrary"),
        name=name,
    )(a1, a2, w, x, gate)


CONV_HALO = 16


def _conv_proj_kernel(a_ref, ap_ref, wb_ref, wc_ref, wu_ref, cw_ref, o_ref, ax_ref):
    i = pl.program_id(0)
    h = CONV_HALO

    @pl.when(pl.program_id(1) == 0)
    def _():
        ax_ref[0:h, :] = ap_ref[...]
        ax_ref[h:, :] = a_ref[...]

    ax = ax_ref[...]
    b = _dot(a_ref[...], wb_ref[...])
    z = _dot(ax, wc_ref[...]) * _dot(ax, wu_ref[...])
    row = lax.broadcasted_iota(jnp.int32, z.shape, 0)
    z = jnp.where(jnp.logical_or(i > 0, row >= h), z, 0.0)
    z1 = pltpu.roll(z, 1, 0)
    z2 = pltpu.roll(z, 2, 0)
    cw = cw_ref[...]
    y = cw[2:3, :] * z[h:, :] + cw[1:2, :] * z1[h:, :] + cw[0:1, :] * z2[h:, :]
    o_ref[...] = (b * y).astype(o_ref.dtype)


def _conv_project(a, w, conv_w, tm=1024, tn=256):
    m, k = a.shape
    n = D_MODEL
    nb = n // tn
    hb = tm // CONV_HALO
    return pl.pallas_call(
        _conv_proj_kernel,
        grid=(m // tm, nb),
        in_specs=[
            pl.BlockSpec((tm, k), lambda i, j: (i, 0), pipeline_mode=pl.Buffered(1)),
            pl.BlockSpec((CONV_HALO, k), lambda i, j: (jnp.maximum(i * hb - 1, 0), 0)),
            pl.BlockSpec((k, tn), lambda i, j: (0, j)),
            pl.BlockSpec((k, tn), lambda i, j: (0, nb + j)),
            pl.BlockSpec((k, tn), lambda i, j: (0, 2 * nb + j)),
            pl.BlockSpec((3, tn), lambda i, j: (0, j)),
        ],
        out_specs=pl.BlockSpec((tm, tn), lambda i, j: (i, j)),
        out_shape=jax.ShapeDtypeStruct((m, n), BF16),
        scratch_shapes=[pltpu.VMEM((tm + CONV_HALO, k), BF16)],
        compiler_params=_params("arbitrary", "arbitrary"),
        name="conv_proj",
    )(a, a, w, w, w, conv_w)


RET_T = 512
RET_HP = 2


def _ret_kernel(lg_ref, q_ref, k_ref, v_ref, g_ref, ca_ref, wm_ref, b_ref, o_ref, mod_ref, state_ref, macc_ref):
    step = pl.program_id(0) * pl.num_programs(1) + pl.program_id(1)
    _mod_ride(step, pl.num_programs(0) * pl.num_programs(1) - 1, ca_ref, wm_ref, b_ref, macc_ref, mod_ref)

    @pl.when(pl.program_id(1) == 0)
    def _():
        state_ref[...] = jnp.zeros_like(state_ref)

    c = RET_CHUNK
    dh = RET_DIM
    inv_scale = dh ** -0.5
    ri = lax.broadcasted_iota(jnp.int32, (c, c), 0)
    ci = lax.broadcasted_iota(jnp.int32, (c, c), 1)
    rel = (ri - ci).astype(F32)
    pos = lax.broadcasted_iota(jnp.int32, (c, 1), 0).astype(F32)
    decays = []
    for hh in range(RET_HP):
        lg = lg_ref[pl.program_id(0) * RET_HP + hh]
        intra = jnp.where(rel >= 0, jnp.exp(lg * jnp.maximum(rel, 0.0)), 0.0) * inv_scale
        q_decay = jnp.exp(lg * (pos + 1.0))
        k_decay = jnp.exp(lg * (c - 1.0 - pos)) * inv_scale
        chunk_decay = jnp.exp(jnp.full((1, 1), lg * c, F32))
        decays.append((intra, q_decay, k_decay, chunk_decay))

    for t in range(RET_T // c):
        rows = slice(t * c, (t + 1) * c)
        for hh in range(RET_HP):
            intra, q_decay, k_decay, chunk_decay = decays[hh]
            cols = slice(hh * dh, (hh + 1) * dh)
            q = q_ref[rows, cols]
            k = k_ref[rows, cols]
            v = v_ref[rows, cols]
            state = state_ref[hh]
            scores = lax.dot_general(q, k, (((1,), (1,)), ((), ())), preferred_element_type=F32) * intra
            o = _dot(scores.astype(BF16), v)
            o = o + _dot(q, state.astype(BF16)) * q_decay
            kd = (k.astype(F32) * k_decay).astype(BF16)
            kv = lax.dot_general(kd, v, (((0,), (0,)), ((), ())), preferred_element_type=F32)
            state_ref[hh] = state * chunk_decay + kv
            mu = jnp.mean(o, axis=-1, keepdims=True)
            oc = o - mu
            var = jnp.mean(oc * oc, axis=-1, keepdims=True)
            on = oc * lax.rsqrt(var + EPS)
            o_ref[rows, cols] = (on * _silu(g_ref[rows, cols].astype(F32))).astype(o_ref.dtype)


def _retention(p_ret, log_g, ca_rep, w_mod, b_mod):
    m = p_ret.shape[0]
    w = RET_HP * RET_DIM
    groups = RET_HEADS // RET_HP
    ns = m // RET_T
    rows = D_MODEL // (groups * ns)
    n = w_mod.shape[1]

    def spec(part):
        return pl.BlockSpec((RET_T, w), lambda g, s: (s, part * groups + g))

    slab = lambda g, s: (g * ns + s, 0)
    whole = lambda g, s: (0, 0)
    return pl.pallas_call(
        _ret_kernel,
        grid=(groups, ns),
        in_specs=[pl.BlockSpec(memory_space=pltpu.SMEM), spec(0), spec(1), spec(2), spec(3),
                  pl.BlockSpec((rows, LANES), slab), pl.BlockSpec((rows, n), slab), pl.BlockSpec((1, n), whole)],
        out_specs=[pl.BlockSpec((RET_T, w), lambda g, s: (s, g)), pl.BlockSpec((1, n), whole)],
        out_shape=[jax.ShapeDtypeStruct((m, RET_W), BF16), jax.ShapeDtypeStruct((1, n), F32)],
        scratch_shapes=[pltpu.VMEM((RET_HP, RET_DIM, RET_DIM), F32), pltpu.VMEM((8, n), F32)],
        compiler_params=_params("arbitrary", "arbitrary"),
        name="retention",
    )(log_g, p_ret, p_ret, p_ret, p_ret, ca_rep, w_mod, b_mod.reshape(1, n))


MOBA_KC = 1024
MOBA_FILL = 2048
MOBA_HP = 2
MOBA_BLOCK_SHIFT = MOBA_BLOCK.bit_length() - 1


def _moba_kernel(sl_ref, q_ref, k_ref, v_ref, *refs, n_cast, n_mods):
    n_in = n_cast + (1 + 2 * n_mods if n_mods else 0)
    o_ref = refs[n_in]
    n_out = 1 + n_cast + n_mods
    ka_ref, va_ref, km_ref, s_ref, acc_ref = refs[n_in + n_out:n_in + n_out + 5]
    macc_refs = refs[n_in + n_out + 5:]
    for t in range(n_cast):
        refs[n_in + 1 + t][...] = refs[t][...].astype(refs[n_in + 1 + t].dtype)
    step = pl.program_id(0) * pl.num_programs(1) + pl.program_id(1)
    last_step = pl.num_programs(0) * pl.num_programs(1) - 1
    for t in range(n_mods):
        wm_ref, b_ref = refs[n_cast + 1 + 2 * t], refs[n_cast + 2 + 2 * t]
        mod_ref = refs[n_in + 1 + n_cast + t]
        _mod_ride(step, last_step, refs[n_cast], wm_ref, b_ref, macc_refs[t], mod_ref)
    i = pl.program_id(1)
    bs = MOBA_BLOCK
    d = MOBA_DIM
    kc = MOBA_KC
    hp = MOBA_HP
    seq = k_ref.shape[0]
    heads = range(hp)

    @pl.when(i == 0)
    def _():
        km_ref[...] = jnp.zeros_like(km_ref)
        fr = MOBA_FILL
        lane = lax.broadcasted_iota(jnp.int32, (fr, d), 1)
        sub = lax.broadcasted_iota(jnp.int32, (d, kc), 0)
        key = lax.broadcasted_iota(jnp.int32, (d, kc), 1)

        def fill(c, carry):
            r = pl.multiple_of(c * fr, fr)
            ones_col = jnp.where(lane == 0, 1.0, 0.0).astype(BF16)
            for hh in heads:
                kb = k_ref[pl.ds(r, fr), hh * d:(hh + 1) * d]
                for t in range(fr // kc):
                    chunk = c * (fr // kc) + t
                    onehot = sub == jnp.right_shift(key + chunk * kc, MOBA_BLOCK_SHIFT)
                    ka_ref[hh, chunk, 0:d, :] = kb[t * kc:(t + 1) * kc, :].astype(F32).T.astype(BF16)
                    ka_ref[hh, chunk, d:2 * d, :] = jnp.where(onehot, 1.0, 0.0).astype(BF16)
                va_ref[hh, pl.ds(r, fr), 0:d] = v_ref[pl.ds(r, fr), hh * d:(hh + 1) * d]
                va_ref[hh, pl.ds(r, fr), d:2 * d] = ones_col
                kmean = jnp.mean(kb.astype(F32).reshape(fr // bs, bs, d), axis=1)
                km_ref[hh, pl.ds(pl.multiple_of(c * (fr // bs), fr // bs), fr // bs), :] = kmean
            return carry

        lax.fori_loop(0, seq // fr, fill, 0)

    scale = d ** -0.5
    slopes = [sl_ref[pl.program_id(0) * hp + hh] for hh in heads]

    nbp = -(-(seq // bs) // 8) * 8
    blk = lax.broadcasted_iota(jnp.int32, (nbp, bs), 0)
    blkf = blk.astype(F32)
    q_augs = []
    for hh in heads:
        q = q_ref[:, hh * d:(hh + 1) * d]
        gate = lax.dot_general(km_ref[hh], q, (((1,), (1,)), ((), ())),
                               precision=lax.Precision.HIGHEST, preferred_element_type=F32)[0:nbp, :]
        g = jnp.where(blk < i, gate, -jnp.inf)
        chosen = blk == i
        for _ in range(MOBA_TOPK):
            mx = jnp.max(g, axis=0, keepdims=True)
            idx = jnp.min(jnp.where(g == mx, blkf, 2.0 * d), axis=0, keepdims=True)
            hit = blkf == idx
            chosen = jnp.logical_or(chosen, jnp.logical_and(hit, mx > -jnp.inf))
            g = jnp.where(hit, -jnp.inf, g)
        bias_t = jnp.concatenate(
            [jnp.where(chosen, 0.0, NEG_BIG), jnp.zeros((d - nbp, bs), F32)], axis=0)
        q_augs.append(jnp.concatenate([q.astype(BF16), bias_t.T.astype(BF16)], axis=1))

    def scores(j, causal, hh):
        r = pl.multiple_of(j * kc, kc)
        s = _dot(q_augs[hh], ka_ref[hh, j]) * scale
        rel = lax.broadcasted_iota(jnp.int32, (1, kc), 1) + (r - i * bs)
        s = s + slopes[hh] * rel.astype(F32)
        if causal:
            row = lax.broadcasted_iota(jnp.int32, (bs, kc), 0)
            s = jnp.where(rel <= row, s, NEG_BIG)
        s_ref[hh, j] = s
        part = s[:, 0:d]
        for t in range(1, kc // d):
            part = jnp.maximum(part, s[:, t * d:(t + 1) * d])
        return part

    def sweep(j, mxs, causal):
        return tuple(jnp.maximum(mxs[hh], scores(j, causal, hh)) for hh in heads)

    n_past = i // (kc // bs)
    mxs = tuple(jnp.full((bs, d), -jnp.inf, F32) for _ in heads)
    mxs = lax.fori_loop(0, n_past, lambda j, c: sweep(j, c, False), mxs)
    mxs = sweep(n_past, mxs, True)
    ms = [jnp.max(mxs[hh], axis=1, keepdims=True) for hh in heads]

    acc_ref[...] = jnp.zeros_like(acc_ref)

    def weighted(j, carry):
        r = pl.multiple_of(j * kc, kc)
        for hh in heads:
            p = jnp.exp(s_ref[hh, j] - ms[hh]).astype(BF16)
            acc_ref[hh] += _dot(p, va_ref[hh, pl.ds(r, kc), :])
        return carry

    lax.fori_loop(0, n_past + 1, weighted, 0)
    for hh in heads:
        acc = acc_ref[hh]
        o_ref[:, hh * d:(hh + 1) * d] = (acc[:, 0:d] / acc[:, d:d + 1]).astype(o_ref.dtype)


MOBA_CAST_COLS = 16


def _moba(q_all, kv_all, slopes, cast_ws=(), ca_rep=None, mods=()):
    m = q_all.shape[0]
    hp = MOBA_HP
    w = hp * MOBA_DIM
    groups = q_all.shape[1] // w
    nq = m // MOBA_BLOCK
    row_blocks = groups * nq // MOBA_CAST_COLS

    def cast_block(g, i):
        step = g * nq + i
        return (step // MOBA_CAST_COLS, step % MOBA_CAST_COLS)

    cast_specs = [pl.BlockSpec((cw.shape[0] // row_blocks, cw.shape[1] // MOBA_CAST_COLS), cast_block)
                  for cw in cast_ws]

    steps = groups * nq
    slab = lambda g, i: (g * nq + i, 0)
    whole = lambda g, i: (0, 0)
    mod_in_specs, mod_out_specs, mod_out_shapes, mod_args, mod_scratch = [], [], [], [], []
    if mods:
        mod_in_specs.append(pl.BlockSpec((D_MODEL // steps, LANES), slab))
        mod_args.append(ca_rep)
    for w_mod, b_mod in mods:
        n = w_mod.shape[1]
        mod_in_specs += [pl.BlockSpec((D_MODEL // steps, n), slab), pl.BlockSpec((1, n), whole)]
        mod_out_specs.append(pl.BlockSpec((1, n), whole))
        mod_out_shapes.append(jax.ShapeDtypeStruct((1, n), F32))
        mod_scratch.append(pltpu.VMEM((8, n), F32))
        mod_args += [w_mod, b_mod.reshape(1, n)]

    def kv_spec(part):
        return pl.BlockSpec((m, w), lambda g, i: (0, part * groups + g), pipeline_mode=pl.Buffered(1))

    staged_keys = pltpu.VMEM((hp, m // MOBA_KC, 2 * MOBA_DIM, MOBA_KC), BF16)
    staged_values = pltpu.VMEM((hp, m, 2 * MOBA_DIM), BF16)
    outs = pl.pallas_call(
        functools.partial(_moba_kernel, n_cast=len(cast_ws), n_mods=len(mods)),
        grid=(groups, nq),
        in_specs=[
            pl.BlockSpec(memory_space=pltpu.SMEM),
            pl.BlockSpec((MOBA_BLOCK, w), lambda g, i: (i, g)),
            kv_spec(0),
            kv_spec(1),
        ] + cast_specs + mod_in_specs,
        out_specs=[pl.BlockSpec((MOBA_BLOCK, w), lambda g, i: (i, g))] + cast_specs + mod_out_specs,
        out_shape=[jax.ShapeDtypeStruct(q_all.shape, BF16)]
        + [jax.ShapeDtypeStruct(cw.shape, BF16) for cw in cast_ws] + mod_out_shapes,
        scratch_shapes=[
            staged_keys,
            staged_values,
            pltpu.VMEM((hp, MOBA_DIM, MOBA_DIM), F32),
            pltpu.VMEM((hp, m // MOBA_KC, MOBA_BLOCK, MOBA_KC), F32),
            pltpu.VMEM((hp, MOBA_BLOCK, 2 * MOBA_DIM), F32),
        ] + mod_scratch,
        compiler_params=_params("arbitrary", "arbitrary"),
        name="moba",
    )(slopes, q_all, kv_all, kv_all, *cast_ws, *mod_args)
    nc = len(cast_ws)
    return outs[0], outs[1:1 + nc], outs[1 + nc:]


def _split_mod(mod):
    d = D_MODEL
    return mod[:, :d], mod[:, d:2 * d], mod[:, 2 * d:]


def _ffn(x, mod, norm_g, w1, w3, w2_bf16):
    shift, scale, gate = _split_mod(mod)
    h = _norm_modulate(x, norm_g, scale, shift)
    u = _swiglu_up(h, w1, w3)
    return _project_residual(u, w2_bf16, x, gate, tm=512, name="ffn_down")


def kernel(x, c, l0_mix_norm_g, l0_mix_mod_w, l0_mix_mod_b, l0_mix_w_in, l0_mix_w_out, l0_ffn_norm_g, l0_ffn_mod_w, l0_ffn_mod_b, l0_ffn_w1, l0_ffn_w3, l0_ffn_w2, l1_mix_norm_g, l1_mix_mod_w, l1_mix_mod_b, l1_mix_w_in, l1_mix_conv_w, l1_mix_w_out, l1_ffn_norm_g, l1_ffn_mod_w, l1_ffn_mod_b, l1_ffn_w1, l1_ffn_w3, l1_ffn_w2, final_norm_g):
    b, seq, d = x.shape
    xs = x.reshape(b * seq, d)
    c_col = c.reshape(d, 1)

    log_g = jnp.log1p(-jnp.exp2(-5.0 - jnp.arange(RET_HEADS, dtype=F32)))
    slopes = jnp.exp2(-8.0 * jnp.arange(1, MOBA_HEADS + 1, dtype=F32) / MOBA_HEADS)

    mod, ca_rep = _mod_vector(c_col, l0_mix_mod_w, l0_mix_mod_b)
    shift, scale, gate = _split_mod(mod)
    h = _norm_modulate(xs, l0_mix_norm_g, scale, shift)
    w_in = l0_mix_w_in
    p_ret = _project(h, w_in, 0, 4 * RET_W, BF16, name="mix0_in_ret")
    moba_q = _project(h, w_in, 4 * RET_W, MOBA_W, F32, name="mix0_in_q")
    moba_kv = _project(h, w_in, 4 * RET_W + MOBA_W, 2 * MOBA_W, BF16, name="mix0_in_kv")
    ret, l1_ffn_mod = _retention(p_ret, log_g, ca_rep, l1_ffn_mod_w, l1_ffn_mod_b)
    att, (l0_w2, l1_w2), (l0_ffn_mod, l1_mix_mod) = _moba(
        moba_q, moba_kv, slopes, (l0_ffn_w2, l1_ffn_w2), ca_rep,
        ((l0_ffn_mod_w, l0_ffn_mod_b), (l1_mix_mod_w, l1_mix_mod_b)))
    xs = _project_residual_cat(ret, att, l0_mix_w_out, xs, gate, name="mix0_out")
    xs = _ffn(xs, l0_ffn_mod, l0_ffn_norm_g, l0_ffn_w1, l0_ffn_w3, l0_w2)

    shift, scale, gate = _split_mod(l1_mix_mod)
    h = _norm_modulate(xs, l1_mix_norm_g, scale, shift)
    s = _conv_project(h, l1_mix_w_in, l1_mix_conv_w.reshape(3, d))
    xs = _project_residual(s, l1_mix_w_out, xs, gate, name="mix1_out")
    xs = _ffn(xs, l1_ffn_mod, l1_ffn_norm_g, l1_ffn_w1, l1_ffn_w3, l1_w2)

    return _final_norm(xs, final_norm_g).reshape(b, seq, d)
```
